```python
import jax
import jax.numpy as jnp
from jax import lax
import numpy as np

D_MODEL = 4096
BATCH = 4
SEQ = 2048
DEPTH = 4
DEC_BATCH = 8
DEC_SEQ = 8
PAST_LEN = 8192
PAGE_SIZE = 128

N_EVEN = (DEPTH + 1) // 2
N_ODD = DEPTH // 2
HEAD_DIM = 128
H_A = D_MODEL // 256
DK_A = HEAD_DIM
DV_A = HEAD_DIM
CONV_A = 4
QA = H_A * DK_A
VA = H_A * DV_A
CONV_A_CH = 2 * QA + VA
H_B = D_MODEL // 512
DK_B = HEAD_DIM
DV_B = 2 * HEAD_DIM
QB = H_B * DK_B
VB = H_B * DV_B
RET_THETA = 10000.0
H_C = D_MODEL // 256
HD_C = HEAD_DIM
QC = H_C * HD_C
H_D = D_MODEL // 512
DK_D = HEAD_DIM
DV_D = 2 * HEAD_DIM
QD = H_D * DK_D
VD = H_D * DV_D
R_D = 16
TAU_D = 16.0
MIX_EVEN = VA + VB
MIX_ODD = QC + VD
W_EVEN = CONV_A_CH + 2 * H_A + VA + 2 * QB + 2 * VB
W_ODD = 3 * QC + 2 * QD + 2 * VD + R_D
FFN_DIM = ((8 * D_MODEL // 3 + 255) // 256) * 256
FFN_CONV = 3
PLE_DIM = 256
CHUNK = 64
QBLK = 128
EPS = 1e-6
F32 = jnp.float32

kernel_name = 'hybrid_delta_retention_stickbreak_gla_step'


def _rms(x, g):
    x32 = x.astype(F32)
    return x32 * lax.rsqrt(jnp.mean(x32 * x32, -1, keepdims=True) + EPS) * g.astype(F32)


def _l2n(x):
    x32 = x.astype(F32)
    return x32 * lax.rsqrt(jnp.sum(x32 * x32, -1, keepdims=True) + EPS)


def _group_norm(x):
    x32 = x.astype(F32)
    xc = x32 - jnp.mean(x32, -1, keepdims=True)
    return xc * lax.rsqrt(jnp.mean(xc * xc, -1, keepdims=True) + EPS)


def _split(x, sizes):
    offs = np.cumsum(sizes)[:-1].tolist()
    return jnp.split(x, offs, axis=-1)


def _causal_conv(x, buf, w):
    K = w.shape[0]
    T = x.shape[1]
    xp = jnp.concatenate([buf.astype(x.dtype), x], axis=1)
    y = xp[:, 0:T] * w[0]
    for j in range(1, K):
        y = y + xp[:, j:j + T] * w[j]
    return y, xp[:, T:]


def _rotate(x, pos):
    half = x.shape[-1] // 2
    inv = 1.0 / (RET_THETA ** (jnp.arange(half, dtype=F32) / half))
    ang = pos.astype(F32)[:, None] * inv[None, :]
    cos = jnp.cos(ang)[None, :, None, :]
    sin = jnp.sin(ang)[None, :, None, :]
    x1, x2 = x[..., :half], x[..., half:]
    return jnp.concatenate([x1 * cos - x2 * sin, x1 * sin + x2 * cos], -1)


def _chunkify(a, C, N):
    pad = N * C - a.shape[1]
    a = jnp.pad(a, [(0, 0), (0, pad)] + [(0, 0)] * (a.ndim - 2))
    return jnp.moveaxis(a.reshape(a.shape[0], N, C, *a.shape[2:]), 1, 0)


def _gated_delta(q, k, v, g, beta, s0):
    Bz, T, H, _ = q.shape
    dv = v.shape[-1]
    C = min(CHUNK, T)
    N = -(-T // C)
    xs = tuple(_chunkify(a, C, N) for a in (q, k, v, g, beta))
    incl = jnp.tril(jnp.ones((C, C), dtype=bool))
    strict = jnp.tril(jnp.ones((C, C), dtype=bool), -1)
    eye = jnp.eye(C, dtype=F32)

    def step(S, inp):
        qc, kc, vc, gc, bc = inp
        G = jnp.cumsum(gc, axis=1).transpose(0, 2, 1)
        diff = G[..., :, None] - G[..., None, :]
        decay = jnp.where(incl, jnp.exp(jnp.where(incl, diff, 0.0)), 0.0)
        qh = qc.transpose(0, 2, 1, 3)
        kh = kc.transpose(0, 2, 1, 3)
        vh = vc.transpose(0, 2, 1, 3)
        bh = bc.transpose(0, 2, 1)
        kk = jnp.einsum('bhik,bhjk->bhij', kh, kh)
        amat = eye + jnp.where(strict, kk * decay * bh[..., :, None], 0.0)
        rhs = jnp.concatenate([vh * bh[..., None], kh * (bh * jnp.exp(G))[..., None]], -1)
        sol = lax.linalg.triangular_solve(amat, rhs, left_side=True, lower=True, unit_diagonal=True)
        u = sol[..., :dv] - jnp.einsum('bhck,bhkv->bhcv', sol[..., dv:], S)
        attn = jnp.einsum('bhik,bhjk->bhij', qh, kh) * decay
        o = jnp.einsum('bhck,bhkv->bhcv', qh * jnp.exp(G)[..., None], S) + jnp.einsum('bhij,bhjv->bhiv', attn, u)
        g_last = G[..., -1]
        k_dec = kh * jnp.exp(g_last[..., None] - G)[..., None]
        S_new = S * jnp.exp(g_last)[..., None, None] + jnp.einsum('bhck,bhcv->bhkv', k_dec, u)
        return S_new, o.transpose(0, 2, 1, 3)

    S, o = lax.scan(step, s0, xs)
    o = jnp.moveaxis(o, 0, 1).reshape(Bz, N * C, H, dv)[:, :T]
    return o, S


def _chunked_gla(q, k, v, logd, s0):
    Bz, T, H, _ = q.shape
    dv = v.shape[-1]
    C = min(CHUNK, T)
    N = -(-T // C)
    xs = tuple(_chunkify(a, C, N) for a in (q, k, v, logd))
    m = jnp.tril(jnp.ones((C, C), dtype=bool))[None, :, :, None, None]

    def step(S, inp):
        qc, kc, vc, lc = inp
        G = jnp.cumsum(lc, axis=1)
        diff = G[:, :, None] - G[:, None, :]
        decay = jnp.where(m, jnp.exp(jnp.where(m, diff, 0.0)), 0.0)
        scores = jnp.einsum('bihk,bjhk,bijhk->bhij', qc, kc, decay)
        o = jnp.einsum('bhij,bjhv->bihv', scores, vc) + jnp.einsum('bihk,bhkv->bihv', qc * jnp.exp(G), S)
        g_last = G[:, -1]
        S_new = S * jnp.exp(g_last)[..., None] + jnp.einsum('bjhk,bjhv->bhkv', kc * jnp.exp(g_last[:, None] - G), vc)
        return S_new, o

    S, o = lax.scan(step, s0, xs)
    o = jnp.moveaxis(o, 0, 1).reshape(Bz, N * C, H, dv)[:, :T]
    return o, S


def _stick_breaking(q, k, v, bias, q_offset):
    T = q.shape[1]
    scale = q.shape[-1] ** -0.5
    b = bias.astype(F32)[None, :, None, None]
    outs = []
    for s0 in range(0, T, QBLK):
        e = min(s0 + QBLK, T)
        nk = q_offset + e
        z = jnp.einsum('bqhd,bkhd->bhqk', q[:, s0:e].astype(F32), k[:, :nk].astype(F32)) * scale + b
        qpos = q_offset + s0 + jnp.arange(e - s0)
        mask = jnp.arange(nk)[None, :] < qpos[:, None]
        log_beta = jax.nn.log_sigmoid(z)
        log_1mb = jnp.where(mask, log_beta - z, 0.0)
        suffix = lax.cumsum(log_1mb, axis=3, reverse=True) - log_1mb
        a = jnp.where(mask, jnp.exp(jnp.where(mask, log_beta + suffix, 0.0)), 0.0)
        outs.append(jnp.einsum('bhqk,bkhd->bqhd', a, v[:, :nk].astype(F32)))
    return jnp.concatenate(outs, axis=1)


def _mixer_even(xn, pos, conv_buf, s_a, s_b, w_in, conv_w, a_log, dt_bias, a_norm, b_gn, w_out):
    Bz, T, _ = xn.shape
    proj = jnp.einsum('btd,de->bte', xn, w_in)
    qkv_a, beta_raw, a_raw, gate_a, q_b, k_b, v_b, gate_b = _split(proj, [CONV_A_CH, H_A, H_A, VA, QB, QB, VB, VB])
    qkv_a, conv_new = _causal_conv(qkv_a, conv_buf, conv_w)
    qkv_a = jax.nn.silu(qkv_a.astype(F32))
    q_a, k_a, v_a = _split(qkv_a, [QA, QA, VA])
    q_a = _l2n(q_a.reshape(Bz, T, H_A, DK_A)) * (DK_A ** -0.5)
    k_a = _l2n(k_a.reshape(Bz, T, H_A, DK_A))
    v_a = v_a.reshape(Bz, T, H_A, DV_A)
    beta = jax.nn.sigmoid(beta_raw.astype(F32))
    g = -jnp.exp(a_log.astype(F32)) * jax.nn.softplus(a_raw.astype(F32) + dt_bias.astype(F32))
    o_a, s_a_new = _gated_delta(q_a, k_a, v_a, g, beta, s_a.astype(F32))
    o_a = _rms(o_a, a_norm) * jax.nn.silu(gate_a.astype(F32)).reshape(Bz, T, H_A, DV_A)
    q_b = _rotate(q_b.reshape(Bz, T, H_B, DK_B).astype(F32), pos)
    k_b = _rotate(k_b.reshape(Bz, T, H_B, DK_B).astype(F32), pos) * (DK_B ** -0.5)
    v_b = v_b.reshape(Bz, T, H_B, DV_B).astype(F32)
    log_gamma = jnp.log1p(-jnp.exp2(-5.0 - jnp.arange(H_B, dtype=F32)))
    logd = jnp.broadcast_to(log_gamma[None, None, :, None], (Bz, T, H_B, DK_B))
    o_b, s_b_new = _chunked_gla(q_b, k_b, v_b, logd, s_b.astype(F32))
    o_b = _group_norm(o_b) * b_gn.astype(F32).reshape(H_B, DV_B) * jax.nn.silu(gate_b.astype(F32)).reshape(Bz, T, H_B, DV_B)
    mix = jnp.concatenate([o_a.reshape(Bz, T, VA), o_b.reshape(Bz, T, VB)], -1)
    out = jnp.einsum('bte,ed->btd', mix.astype(w_out.dtype), w_out)
    return out, conv_new, s_a_new, s_b_new


def _mixer_odd(xn, past_k, past_v, s_d, w_in, q_norm, k_norm, sb_bias, alpha_w2, alpha_b, d_norm, w_out):
    Bz, T, _ = xn.shape
    proj = jnp.einsum('btd,de->bte', xn, w_in)
    q_c, k_c, v_c, q_d, k_d, v_d, gate_d, alpha_lr = _split(proj, [QC, QC, QC, QD, QD, VD, VD, R_D])
    q_c = _rms(q_c.reshape(Bz, T, H_C, HD_C), q_norm)
    k_c = _rms(k_c.reshape(Bz, T, H_C, HD_C), k_norm)
    v_c = v_c.reshape(Bz, T, H_C, HD_C).astype(F32)
    k_all = jnp.concatenate([past_k.astype(F32), k_c], axis=1)
    v_all = jnp.concatenate([past_v.astype(F32), v_c], axis=1)
    o_c = _stick_breaking(q_c, k_all, v_all, sb_bias, past_k.shape[1])
    q_d = q_d.reshape(Bz, T, H_D, DK_D).astype(F32)
    k_d = k_d.reshape(Bz, T, H_D, DK_D).astype(F32) * (DK_D ** -0.5)
    v_d = v_d.reshape(Bz, T, H_D, DV_D).astype(F32)
    a_logit = jnp.einsum('btr,re->bte', alpha_lr, alpha_w2) + alpha_b
    logd = (jax.nn.log_sigmoid(a_logit.astype(F32)) / TAU_D).reshape(Bz, T, H_D, DK_D)
    o_d, s_d_new = _chunked_gla(q_d, k_d, v_d, logd, s_d.astype(F32))
    o_d = _rms(o_d, d_norm) * jax.nn.silu(gate_d.astype(F32)).reshape(Bz, T, H_D, DV_D)
    mix = jnp.concatenate([o_c.reshape(Bz, T, QC), o_d.reshape(Bz, T, VD)], -1)
    out = jnp.einsum('bte,ed->btd', mix.astype(w_out.dtype), w_out)
    return out, k_c, v_c, s_d_new


def _conv_ffn(xn, buf, w_in, conv_w, w_out):
    u = jnp.einsum('btd,df->btf', xn, w_in)
    a, b = _split(u, [FFN_DIM, FFN_DIM])
    a_c, buf_new = _causal_conv(a, buf, conv_w)
    hid = jax.nn.gelu(a_c.astype(F32), approximate=True) * b.astype(F32)
    return jnp.einsum('btf,fd->btd', hid.astype(w_out.dtype), w_out), buf_new


def _trunk(x, p, pos, conv_a, rec_a, rec_b, past_kv, rec_d, ffn_buf, P):
    h = x
    conv_l, ra_l, rb_l, k_l, v_l, rd_l, ffn_l = [], [], [], [], [], [], []
    for i in range(DEPTH):
        j = i // 2
        xn = _rms(h, P['norm_mix'][i]).astype(h.dtype)
        if i % 2 == 0:
            out, c_new, sa_new, sb_new = _mixer_even(
                xn, pos, conv_a[j], rec_a[j], rec_b[j], P['w_in_even'][j], P['conv_a_w'][j],
                P['a_log'][j], P['dt_bias'][j], P['a_out_norm'][j], P['b_gn'][j], P['w_out_even'][j])
            conv_l.append(c_new)
            ra_l.append(sa_new)
            rb_l.append(sb_new)
        else:
            out, k_new, v_new, sd_new = _mixer_odd(
                xn, past_kv[j][0], past_kv[j][1], rec_d[j], P['w_in_odd'][j], P['c_q_norm'][j],
                P['c_k_norm'][j], P['c_logit_bias'][j], P['d_alpha_w2'][j], P['d_alpha_b'][j],
                P['d_out_norm'][j], P['w_out_odd'][j])
            k_l.append(k_new)
            v_l.append(v_new)
            rd_l.append(sd_new)
        h = h + out.astype(h.dtype)
        xn = _rms(h, P['norm_ffn'][i]).astype(h.dtype)
        f, fb = _conv_ffn(xn, ffn_buf[i], P['w_ffn_in'][i], P['ffn_conv_w'][i], P['w_ffn_out'][i])
        ffn_l.append(fb)
        h = h + f.astype(h.dtype)
        gate = jax.nn.sigmoid(jnp.einsum('btd,de->bte', _rms(h, P['ple_norm'][i]).astype(h.dtype), P['w_ple_gate'][i]).astype(F32))
        emb = jnp.einsum('btq,qd->btd', p[i], P['w_ple_in'][i]).astype(F32)
        h = h + (gate * emb).astype(h.dtype)
    return (h, jnp.stack(conv_l), jnp.stack(ra_l), jnp.stack(rb_l), jnp.stack(k_l), jnp.stack(v_l),
            jnp.stack(rd_l), jnp.stack(ffn_l))


def setup_inputs(seed: int = 0) -> dict:
    key = jax.random.key(seed)
    keys = list(jax.random.split(key, 40))

    def nrm(shape, scale=1.0):
        return jax.random.normal(keys.pop(), shape, F32) * scale

    def gain(shape):
        return 1.0 + 0.02 * jax.random.normal(keys.pop(), shape, F32)

    n_pages = PAST_LEN // PAGE_SIZE
    n_used = DEC_BATCH * n_pages
    n_pool = (5 * n_used + 3) // 4
    page_table = jax.random.permutation(keys.pop(), n_pool)[:n_used].reshape(DEC_BATCH, n_pages).astype(jnp.int32)
    a_log = jnp.log(jax.random.uniform(keys.pop(), (N_EVEN, H_A), F32, 1.0, 16.0))
    dt = jnp.exp(jax.random.uniform(keys.pop(), (N_EVEN, H_A), F32, float(np.log(1e-3)), float(np.log(1e-1))))
    dt_bias = dt + jnp.log(-jnp.expm1(-dt))
    c_logit_bias = jax.random.uniform(keys.pop(), (N_ODD, H_C), F32, -8.0, -5.0)
    return {
        'x_prompt': nrm((BATCH, SEQ, D_MODEL)),
        'x_sample': nrm((DEC_BATCH, DEC_SEQ, D_MODEL)),
        'state_a_conv': nrm((N_EVEN, DEC_BATCH, CONV_A - 1, CONV_A_CH)),
        'state_a_rec': nrm((N_EVEN, DEC_BATCH, H_A, DK_A, DV_A), 0.1),
        'state_b_rec': nrm((N_EVEN, DEC_BATCH, H_B, DK_B, DV_B), 0.5),
        'cache_k': nrm((N_ODD, n_pool, PAGE_SIZE, H_C, HD_C)),
        'cache_v': nrm((N_ODD, n_pool, PAGE_SIZE, H_C, HD_C)),
        'state_d_rec': nrm((N_ODD, DEC_BATCH, H_D, DK_D, DV_D), 0.5),
        'state_ffn_conv': nrm((DEPTH, DEC_BATCH, FFN_CONV - 1, FFN_DIM)),
        'page_table': page_table,
        'p_prompt': nrm((DEPTH, BATCH, SEQ, PLE_DIM)),
        'p_sample': nrm((DEPTH, DEC_BATCH, DEC_SEQ, PLE_DIM)),
        'norm_mix': gain((DEPTH, D_MODEL)),
        'w_in_even': nrm((N_EVEN, D_MODEL, W_EVEN), D_MODEL ** -0.5),
        'conv_a_w': nrm((N_EVEN, CONV_A, CONV_A_CH), CONV_A ** -0.5),
        'a_log': a_log,
        'dt_bias': dt_bias,
        'a_out_norm': gain((N_EVEN, DV_A)),
        'b_gn': gain((N_EVEN, VB)),
        'w_out_even': nrm((N_EVEN, MIX_EVEN, D_MODEL), MIX_EVEN ** -0.5),
        'w_in_odd': nrm((N_ODD, D_MODEL, W_ODD), D_MODEL ** -0.5),
        'c_q_norm': gain((N_ODD, HD_C)),
        'c_k_norm': gain((N_ODD, HD_C)),
        'c_logit_bias': c_logit_bias,
        'd_alpha_w2': nrm((N_ODD, R_D, QD), R_D ** -0.5),
        'd_alpha_b': nrm((N_ODD, QD), 0.1),
        'd_out_norm': gain((N_ODD, DV_D)),
        'w_out_odd': nrm((N_ODD, MIX_ODD, D_MODEL), MIX_ODD ** -0.5),
        'norm_ffn': gain((DEPTH, D_MODEL)),
        'w_ffn_in': nrm((DEPTH, D_MODEL, 2 * FFN_DIM), D_MODEL ** -0.5),
        'ffn_conv_w': nrm((DEPTH, FFN_CONV, FFN_DIM), FFN_CONV ** -0.5),
        'w_ffn_out': nrm((DEPTH, FFN_DIM, D_MODEL), FFN_DIM ** -0.5),
        'ple_norm': gain((DEPTH, D_MODEL)),
        'w_ple_gate': nrm((DEPTH, D_MODEL, D_MODEL), D_MODEL ** -0.5),
        'w_ple_in': nrm((DEPTH, PLE_DIM, D_MODEL), PLE_DIM ** -0.5),
    }


def reference(x_prompt, x_sample, state_a_conv, state_a_rec, state_b_rec, cache_k, cache_v, state_d_rec,
              state_ffn_conv, page_table, p_prompt, p_sample, norm_mix, w_in_even, conv_a_w, a_log, dt_bias,
              a_out_norm, b_gn, w_out_even, w_in_odd, c_q_norm, c_k_norm, c_logit_bias, d_alpha_w2, d_alpha_b,
              d_out_norm, w_out_odd, norm_ffn, w_ffn_in, ffn_conv_w, w_ffn_out, ple_norm, w_ple_gate, w_ple_in):
    P = {'norm_mix': norm_mix, 'w_in_even': w_in_even, 'conv_a_w': conv_a_w, 'a_log': a_log, 'dt_bias': dt_bias,
         'a_out_norm': a_out_norm, 'b_gn': b_gn, 'w_out_even': w_out_even, 'w_in_odd': w_in_odd,
         'c_q_norm': c_q_norm, 'c_k_norm': c_k_norm, 'c_logit_bias': c_logit_bias, 'd_alpha_w2': d_alpha_w2,
         'd_alpha_b': d_alpha_b, 'd_out_norm': d_out_norm, 'w_out_odd': w_out_odd, 'norm_ffn': norm_ffn,
         'w_ffn_in': w_ffn_in, 'ffn_conv_w': ffn_conv_w, 'w_ffn_out': w_ffn_out, 'ple_norm': ple_norm,
         'w_ple_gate': w_ple_gate, 'w_ple_in': w_ple_in}
    bp, tp = x_prompt.shape[0], x_prompt.shape[1]
    pos_p = jnp.arange(tp, dtype=jnp.int32)
    conv0 = jnp.zeros((N_EVEN, bp, CONV_A - 1, CONV_A_CH), F32)
    ra0 = jnp.zeros((N_EVEN, bp, H_A, DK_A, DV_A), F32)
    rb0 = jnp.zeros((N_EVEN, bp, H_B, DK_B, DV_B), F32)
    rd0 = jnp.zeros((N_ODD, bp, H_D, DK_D, DV_D), F32)
    ffn0 = jnp.zeros((DEPTH, bp, FFN_CONV - 1, FFN_DIM), F32)
    empty = jnp.zeros((bp, 0, H_C, HD_C), F32)
    past_p = [(empty, empty) for _ in range(N_ODD)]
    (y_prompt, a_conv_p, a_rec_p, b_rec_p, k_p, v_p, d_rec_p, ffn_p) = _trunk(
        x_prompt, p_prompt, pos_p, conv0, ra0, rb0, past_p, rd0, ffn0, P)
    bs, ts = x_sample.shape[0], x_sample.shape[1]
    past_len = page_table.shape[1] * cache_k.shape[2]
    pos_s = past_len + jnp.arange(ts, dtype=jnp.int32)
    past_s = []
    for j in range(N_ODD):
        pk = jnp.take(cache_k[j], page_table, axis=0).reshape(bs, past_len, H_C, HD_C)
        pv = jnp.take(cache_v[j], page_table, axis=0).reshape(bs, past_len, H_C, HD_C)
        past_s.append((pk, pv))
    (y_sample, a_conv_s, a_rec_s, b_rec_s, k_s, v_s, d_rec_s, ffn_s) = _trunk(
        x_sample, p_sample, pos_s, state_a_conv, state_a_rec, state_b_rec, past_s, state_d_rec, state_ffn_conv, P)
    return (y_prompt, y_sample,
            a_conv_p.astype(state_a_conv.dtype), a_conv_s.astype(state_a_conv.dtype),
            a_rec_p.astype(state_a_rec.dtype), a_rec_s.astype(state_a_rec.dtype),
            b_rec_p.astype(state_b_rec.dtype), b_rec_s.astype(state_b_rec.dtype),
            k_p.astype(cache_k.dtype), k_s.astype(cache_k.dtype),
            v_p.astype(cache_v.dtype), v_s.astype(cache_v.dtype),
            d_rec_p.astype(state_d_rec.dtype), d_rec_s.astype(state_d_rec.dtype),
            ffn_p.astype(state_ffn_conv.dtype), ffn_s.astype(state_ffn_conv.dtype))
```

```python
import functools
import math

import jax
import jax.numpy as jnp
from jax import lax
from jax.experimental import pallas as pl
from jax.experimental.pallas import tpu as pltpu

F32 = jnp.float32
BF16 = jnp.bfloat16
HP = lax.Precision.HIGHEST
EPS = 1e-6
LANES = 128
BF16_SUBLANES = 16
CHUNK = 64
SUB = 16
RET_THETA = 10000.0
TAU_D = 16.0
VMEM_LIMIT = 50 * 1024 * 1024


def _cparams(*sem):
    return pltpu.CompilerParams(dimension_semantics=sem, vmem_limit_bytes=VMEM_LIMIT)


def _pick(n, pref, mult=LANES):
    t = (min(pref, n) // mult) * mult
    while t >= mult:
        if n % t == 0:
            return t
        t -= mult
    return n


def _act_dtype(t):
    return BF16 if t % BF16_SUBLANES == 0 else F32


def _dot(a, b, prec=None):
    return jnp.dot(a, b, precision=prec, preferred_element_type=F32)


def _dot_nt(a, b, prec=None):
    return lax.dot_general(a, b, (((1,), (1,)), ((), ())), precision=prec, preferred_element_type=F32)


def _dot_tn(a, b, prec=None):
    return lax.dot_general(a, b, (((0,), (0,)), ((), ())), precision=prec, preferred_element_type=F32)


def _silu(x):
    return x * jax.nn.sigmoid(x)


def _softplus(x):
    return jnp.maximum(x, 0.0) + jnp.log1p(jnp.exp(-jnp.abs(x)))


def _log_sigmoid(x):
    return jnp.minimum(x, 0.0) - jnp.log1p(jnp.exp(-jnp.abs(x)))


def _gelu_tanh(x):
    c = math.sqrt(2.0 / math.pi)
    return 0.5 * x * (1.0 + jnp.tanh(c * (x + 0.044715 * (x * x * x))))


def _iota2(shape, dim):
    return lax.broadcasted_iota(jnp.int32, shape, dim)


def _rms_cast(x2, g):
    m, d = x2.shape
    tm = _pick(m, 256, 8)

    def kern(x_ref, g_ref, o_ref):
        x = x_ref[...]
        r = lax.rsqrt(jnp.mean(x * x, axis=-1, keepdims=True) + EPS)
        o_ref[...] = (x * r * g_ref[...]).astype(o_ref.dtype)

    return pl.pallas_call(
        kern,
        grid=(m // tm,),
        in_specs=[pl.BlockSpec((tm, d), lambda i: (i, 0)), pl.BlockSpec((1, d), lambda i: (0, 0))],
        out_specs=pl.BlockSpec((tm, d), lambda i: (i, 0)),
        out_shape=jax.ShapeDtypeStruct((m, d), BF16),
        compiler_params=_cparams("parallel"),
        name="rms_cast",
    )(x2, g.reshape(1, d))


def _matmul(x, w, *, res=None, ple=None, tm_pref=1024, tn_pref=512, tk=None):
    m, kdim = x.shape
    n = w.shape[1]
    tm = _pick(m, tm_pref, 8)
    tn = _pick(n, tn_pref)
    tk = kdim if tk is None else tk
    nk = kdim // tk
    assert kdim % tk == 0 and m % tm == 0 and n % tn == 0

    in_specs = [pl.BlockSpec((tm, tk), lambda i, j, k: (i, k)), pl.BlockSpec((tk, tn), lambda i, j, k: (k, j))]
    args = [x, w]
    if res is not None:
        in_specs.append(pl.BlockSpec((tm, tn), lambda i, j, k: (i, j)))
        args.append(res)
    if ple is not None:
        h, p, w_in = ple
        pdim = p.shape[1]
        in_specs += [pl.BlockSpec((tm, tn), lambda i, j, k: (i, j)),
                     pl.BlockSpec((tm, pdim), lambda i, j, k: (i, 0)),
                     pl.BlockSpec((pdim, tn), lambda i, j, k: (0, j))]
        args += [h, p, w_in]

    def kern(*refs):
        x_ref, w_ref = refs[0], refs[1]
        o_ref = refs[len(args)]
        acc_ref = refs[len(args) + 1] if nk > 1 else None

        def epilogue(acc):
            if res is not None:
                return refs[2][...] + acc
            if ple is not None:
                h_ref, p_ref, wi_ref = refs[2], refs[3], refs[4]
                emb = _dot(p_ref[...], wi_ref[...])
                return h_ref[...] + jax.nn.sigmoid(acc) * emb
            return acc

        part = _dot(x_ref[...], w_ref[...])
        if nk == 1:
            o_ref[...] = epilogue(part)
        else:
            k = pl.program_id(2)

            @pl.when(k == 0)
            def _():
                acc_ref[...] = part

            @pl.when(jnp.logical_and(k > 0, k < nk - 1))
            def _():
                acc_ref[...] += part

            @pl.when(k == nk - 1)
            def _():
                o_ref[...] = epilogue(acc_ref[...] + part)

    return pl.pallas_call(
        kern,
        grid=(m // tm, n // tn, nk),
        in_specs=in_specs,
        out_specs=pl.BlockSpec((tm, tn), lambda i, j, k: (i, j)),
        out_shape=jax.ShapeDtypeStruct((m, n), F32),
        scratch_shapes=[pltpu.VMEM((tm, tn), F32)] if nk > 1 else [],
        compiler_params=_cparams("parallel", "parallel", "arbitrary"),
        name="matmul",
    )(*args)


def _load_chunk(ref, r0, t):
    if t >= CHUNK:
        return ref[pl.ds(r0, CHUNK), :]
    x = ref[...]
    return jnp.concatenate([x, jnp.zeros((CHUNK - t, x.shape[1]), x.dtype)], axis=0)


def _store_chunk(ref, r0, t, val):
    if t >= CHUNK:
        ref[pl.ds(r0, CHUNK), :] = val.astype(ref.dtype)
    else:
        ref[...] = val[:t].astype(ref.dtype)


def _inv_unit_lower(a):
    c = a.shape[0]
    row = _iota2((c, c), 0)
    col = _iota2((c, c), 1)
    same = (row // SUB) == (col // SUB)
    eye = (row == col).astype(F32)
    a_d = jnp.where(same, a, 0.0)
    low = a - a_d
    t = -a_d
    dinv = eye + t
    p = t
    s = 2
    while s < SUB:
        p = _dot(p, p, HP)
        dinv = dinv + _dot(dinv, p, HP)
        s *= 2
    nb = c // SUB
    if nb == 1:
        return dinv
    nmat = _dot(dinv, low, HP)
    acc = eye - nmat
    pw = nmat
    sign = -1.0
    for _ in range(2, nb):
        pw = _dot(pw, nmat, HP)
        sign = -sign
        acc = acc + sign * pw
    return _dot(acc, dinv, HP)


def _conv_a_prep(proj3, buf, conv_w, *, qa, dk):
    b, t, _ = proj3.shape
    kc, ch = conv_w.shape
    tc = _pick(qa, 512)
    nqb = qa // tc
    assert ch % tc == 0
    off = 8 - (kc - 1)

    def kern(x_ref, buf_ref, w_ref, o_ref, cn_ref, xs):
        j = pl.program_id(1)
        xs[off:8, :] = buf_ref[...]
        xs[8:8 + t, :] = x_ref[...]
        cn_ref[...] = xs[t + off:t + 8, :]
        is_q = j < nqb
        is_k = j < 2 * nqb
        scale = jnp.where(is_q, dk ** -0.5, 1.0).astype(F32)
        for hh in range(tc // LANES):
            sl = slice(hh * LANES, (hh + 1) * LANES)
            y = xs[off:off + t, sl] * w_ref[0:1, sl]
            for kk in range(1, kc):
                y = y + xs[off + kk:off + kk + t, sl] * w_ref[kk:kk + 1, sl]
            s = _silu(y)
            nrm = s * lax.rsqrt(jnp.sum(s * s, axis=-1, keepdims=True) + EPS) * scale
            o_ref[:, sl] = jnp.where(is_k, nrm, s)

    return pl.pallas_call(
        kern,
        grid=(b, ch // tc),
        in_specs=[pl.BlockSpec((None, t, tc), lambda i, j: (i, 0, j)),
                  pl.BlockSpec((None, kc - 1, tc), lambda i, j: (i, 0, j)),
                  pl.BlockSpec((kc, tc), lambda i, j: (0, j))],
        out_specs=[pl.BlockSpec((None, t, tc), lambda i, j: (i, 0, j)),
                   pl.BlockSpec((None, kc - 1, tc), lambda i, j: (i, 0, j))],
        out_shape=[jax.ShapeDtypeStruct((b, t, ch), F32), jax.ShapeDtypeStruct((b, kc - 1, ch), F32)],
        scratch_shapes=[pltpu.VMEM((t + 8, tc), F32)],
        compiler_params=_cparams("parallel", "parallel"),
        name="conv_a_prep",
    )(proj3, buf, conv_w)


def _gated_delta(qkv3, sp3, proj3, gate_blk0, alog_p, dtb_p, a_norm, s0, *, nh):
    b, t, _ = qkv3.shape
    c = CHUNK
    nchunks = max(t // c, 1)
    assert t % c == 0 or t < c
    odt = _act_dtype(t)

    def kern(q_ref, k_ref, v_ref, sp_ref, gate_ref, alog_ref, dtb_ref, an_ref, s0_ref, o_ref, s_ref):
        h = pl.program_id(1)
        row = _iota2((c, c), 0)
        col = _iota2((c, c), 1)
        incl = col <= row
        strict = col < row
        lincl = incl.astype(F32)
        eye = (row == col).astype(F32)
        lane = _iota2((c, LANES), 1)
        rvalid = _iota2((c, 1), 0) < t
        neg_a = -jnp.exp(alog_ref[...])
        dtb = dtb_ref[...]
        an = an_ref[...]
        s_ref[...] = s0_ref[...]

        def chunk(ci, carry):
            r0 = pl.multiple_of(ci * c, c)
            q = _load_chunk(q_ref, r0, t)
            k = _load_chunk(k_ref, r0, t)
            v = _load_chunk(v_ref, r0, t)
            sp = _load_chunk(sp_ref, r0, t)
            beta_all = jax.nn.sigmoid(sp)
            g_all = neg_a * _softplus(sp + dtb)
            bcol = jnp.sum(jnp.where(lane == h, beta_all, 0.0), axis=1, keepdims=True)
            bcol = jnp.where(rvalid, bcol, 0.0)
            g_m = jnp.where(jnp.logical_and(lane == nh + h, rvalid), g_all, 0.0)
            gcol = jnp.sum(_dot(lincl, g_m, HP), axis=1, keepdims=True)
            grow = jnp.sum(eye * gcol, axis=0, keepdims=True)
            diff = gcol - grow
            decay = jnp.where(incl, jnp.exp(jnp.where(incl, diff, 0.0)), 0.0)
            eg = jnp.exp(gcol)
            kk = _dot_nt(k, k, HP)
            amat = jnp.where(strict, kk * decay * bcol, 0.0)
            inv = _inv_unit_lower(amat)
            sol_v = _dot(inv, v * bcol, HP)
            sol_k = _dot(inv, k * (bcol * eg), HP)
            s_old = s_ref[...]
            u = sol_v - _dot(sol_k, s_old, HP)
            attn = _dot_nt(q, k, HP) * decay
            o = _dot(q * eg, s_old, HP) + _dot(attn, u, HP)
            g_last = gcol[c - 1:c, :]
            k_dec = k * jnp.exp(g_last - gcol)
            s_ref[...] = s_old * jnp.exp(g_last) + _dot_tn(k_dec, u, HP)
            on = o * lax.rsqrt(jnp.mean(o * o, axis=-1, keepdims=True) + EPS) * an
            gate = _load_chunk(gate_ref, r0, t)
            _store_chunk(o_ref, r0, t, on * _silu(gate))
            return carry

        if nchunks == 1:
            chunk(0, 0)
        else:
            lax.fori_loop(0, nchunks, chunk, 0)

    blk = lambda off: pl.BlockSpec((None, t, LANES), lambda i, j: (i, 0, off + j))
    return pl.pallas_call(
        kern,
        grid=(b, nh),
        in_specs=[blk(0), blk(nh), blk(2 * nh),
                  pl.BlockSpec((None, t, LANES), lambda i, j: (i, 0, 0)),
                  blk(gate_blk0),
                  pl.BlockSpec((1, LANES), lambda i, j: (0, 0)),
                  pl.BlockSpec((1, LANES), lambda i, j: (0, 0)),
                  pl.BlockSpec((1, LANES), lambda i, j: (0, 0)),
                  pl.BlockSpec((None, None, LANES, LANES), lambda i, j: (i, j, 0, 0))],
        out_specs=[blk(0), pl.BlockSpec((None, None, LANES, LANES), lambda i, j: (i, j, 0, 0))],
        out_shape=[jax.ShapeDtypeStruct((b, t, nh * LANES), odt),
                   jax.ShapeDtypeStruct((b, nh, LANES, LANES), F32)],
        compiler_params=_cparams("parallel", "parallel"),
        name="gated_delta",
    )(qkv3, qkv3, qkv3, sp3, proj3, alog_p, dtb_p, a_norm, s0)


def _gla(proj3, q_blk0, k_blk0, v_blk0, gate_blk0, gain, s0, *, nh, dv, k_scale, norm,
         rot=None, ld3=None, lowrank=None):
    b, t, _ = proj3.shape
    c = CHUNK
    sb = SUB
    nsb = c // sb
    nchunks = max(t // c, 1)
    assert t % c == 0 or t < c
    dk = LANES
    odt = _act_dtype(t)
    n_in = 5 + (2 if rot is not None else 0) + (1 if ld3 is not None else 0) + (3 if lowrank is not None else 0)

    def kern(*refs):
        q_ref, k_ref, v_ref, gate_ref, gain_ref = refs[:5]
        pos = 5
        if rot is not None:
            cos_ref, sin_ref = refs[pos], refs[pos + 1]
            pos += 2
        if ld3 is not None:
            ld_ref = refs[pos]
            pos += 1
        if lowrank is not None:
            sp_ref, w2_ref, ab_ref = refs[pos:pos + 3]
            pos += 3
        s0_ref = refs[pos]
        o_ref, s_ref = refs[n_in + 1], refs[n_in + 2]
        st_ref = refs[n_in + 3]

        row = _iota2((c, c), 0)
        col = _iota2((c, c), 1)
        lincl = (col <= row).astype(F32)
        same = (row // sb) == (col // sb)
        diag_mask = jnp.logical_and(same, col <= row)
        below = (col // sb) < (row // sb)
        rvalid = _iota2((c, 1), 0) < t
        gain_v = gain_ref[...]
        st_ref[...] = s0_ref[...].T

        def chunk(ci, carry):
            r0 = pl.multiple_of(ci * c, c)
            q = _load_chunk(q_ref, r0, t)
            k = _load_chunk(k_ref, r0, t)
            v = _load_chunk(v_ref, r0, t)
            if rot is not None:
                cs = _load_chunk(cos_ref, r0, t)
                sn = _load_chunk(sin_ref, r0, t)
                q = q * cs + pltpu.roll(q, dk // 2, axis=1) * sn
                k = k * cs + pltpu.roll(k, dk // 2, axis=1) * sn
            k = k * k_scale
            if ld3 is not None:
                ld = _load_chunk(ld_ref, r0, t)
            else:
                sp = _load_chunk(sp_ref, r0, t)
                logit = _dot(sp, w2_ref[...], HP) + ab_ref[...]
                ld = jnp.where(rvalid, _log_sigmoid(logit) / TAU_D, 0.0)
            g = _dot(lincl, ld, HP)
            st = st_ref[...]
            o = _dot_nt(q * jnp.exp(g), st, HP)
            offs = [jnp.zeros((sb, c), F32)]
            for a in range(1, nsb):
                gs = g[a * sb - 1:a * sb, :]
                qt = q[a * sb:(a + 1) * sb, :] * jnp.exp(g[a * sb:(a + 1) * sb, :] - gs)
                kt = k * jnp.exp(jnp.minimum(gs - g, 0.0))
                offs.append(_dot_nt(qt, kt, HP))
            s_off = jnp.concatenate(offs, axis=0)
            g3 = g.reshape(nsb, sb, dk)
            q3 = q.reshape(nsb, sb, dk)
            k3 = k.reshape(nsb, sb, dk)
            pd = jnp.zeros((c, c), F32)
            for j in range(sb):
                e = jnp.exp(jnp.minimum(g3 - g3[:, j:j + 1, :], 0.0))
                colj = jnp.sum(q3 * k3[:, j:j + 1, :] * e, axis=-1, keepdims=True).reshape(c, 1)
                pd = jnp.where(jnp.logical_and(same, (col % sb) == j), colj, pd)
            scores = jnp.where(diag_mask, pd, jnp.where(below, s_off, 0.0))
            o = o + _dot(scores, v, HP)
            g_last = g[c - 1:c, :]
            k_dec = k * jnp.exp(g_last - g)
            st_ref[...] = st * jnp.exp(g_last) + _dot_tn(v, k_dec, HP)
            if norm == "group":
                oc = o - jnp.mean(o, axis=-1, keepdims=True)
                on = oc * lax.rsqrt(jnp.mean(oc * oc, axis=-1, keepdims=True) + EPS)
            else:
                on = o * lax.rsqrt(jnp.mean(o * o, axis=-1, keepdims=True) + EPS)
            gate = _load_chunk(gate_ref, r0, t)
            _store_chunk(o_ref, r0, t, on * gain_v * _silu(gate))
            return carry

        if nchunks == 1:
            chunk(0, 0)
        else:
            lax.fori_loop(0, nchunks, chunk, 0)
        s_ref[...] = st_ref[...].T

    def blk(off, width):
        return pl.BlockSpec((None, t, width), lambda i, j: (i, 0, off + j))

    in_specs = [blk(q_blk0, dk), blk(k_blk0, dk), blk(v_blk0, dv), blk(gate_blk0, dv),
                pl.BlockSpec((None, 1, dv), lambda i, j: (j, 0, 0))]
    args = [proj3, proj3, proj3, proj3, gain]
    if rot is not None:
        in_specs += [pl.BlockSpec((t, dk), lambda i, j: (0, 0))] * 2
        args += list(rot)
    if ld3 is not None:
        in_specs.append(blk(0, dk))
        args.append(ld3)
    if lowrank is not None:
        sp3, w2p, ab = lowrank
        in_specs += [pl.BlockSpec((None, t, LANES), lambda i, j: (i, 0, 0)),
                     pl.BlockSpec((LANES, dk), lambda i, j: (0, j)),
                     pl.BlockSpec((1, dk), lambda i, j: (0, j))]
        args += [sp3, w2p, ab]
    in_specs.append(pl.BlockSpec((None, None, dk, dv), lambda i, j: (i, j, 0, 0)))
    args.append(s0)
    assert len(args) == n_in + 1

    return pl.pallas_call(
        kern,
        grid=(b, nh),
        in_specs=in_specs,
        out_specs=[blk(0, dv), pl.BlockSpec((None, None, dk, dv), lambda i, j: (i, j, 0, 0))],
        out_shape=[jax.ShapeDtypeStruct((b, t, nh * dv), odt),
                   jax.ShapeDtypeStruct((b, nh, dk, dv), F32)],
        scratch_shapes=[pltpu.VMEM((dv, dk), F32)],
        compiler_params=_cparams("parallel", "parallel"),
        name="gla_" + norm,
    )(*args)


def _head_rms(proj3, blk0, gain, *, nh):
    b, t, _ = proj3.shape
    width = nh * LANES
    tc = _pick(width, 512)
    per = tc // LANES
    c0 = blk0 // per
    assert blk0 % per == 0

    def kern(x_ref, g_ref, o_ref):
        g = g_ref[...]
        for hh in range(per):
            sl = slice(hh * LANES, (hh + 1) * LANES)
            x = x_ref[:, sl]
            o_ref[:, sl] = x * lax.rsqrt(jnp.mean(x * x, axis=-1, keepdims=True) + EPS) * g

    return pl.pallas_call(
        kern,
        grid=(b, width // tc),
        in_specs=[pl.BlockSpec((None, t, tc), lambda i, j: (i, 0, c0 + j)),
                  pl.BlockSpec((1, LANES), lambda i, j: (0, 0))],
        out_specs=pl.BlockSpec((None, t, tc), lambda i, j: (i, 0, j)),
        out_shape=jax.ShapeDtypeStruct((b, t, width), F32),
        compiler_params=_cparams("parallel", "parallel"),
        name="head_rms",
    )(proj3, gain.reshape(1, LANES))


def _split3(x):
    hi = x.astype(BF16)
    r1 = x - hi.astype(F32)
    mid = r1.astype(BF16)
    lo = (r1 - mid.astype(F32)).astype(BF16)
    return hi, mid, lo


def _suffix_sum(l1, u):
    hi, mid, lo = _split3(l1)
    return _dot(hi, u) + _dot(mid, u) + _dot(lo, u)


def _sb_weights(z, mask, r_run, u):
    lb = _log_sigmoid(z)
    l1 = lb - z
    if mask is not None:
        l1 = jnp.where(mask, l1, 0.0)
    suf = _suffix_sum(l1, u) + r_run
    if mask is not None:
        a = jnp.where(mask, jnp.exp(jnp.where(mask, lb + suf, 0.0)), 0.0)
    else:
        a = jnp.exp(lb + suf)
    return a, jnp.sum(l1, axis=1, keepdims=True)


def _sb_prompt(qn3, kn3, proj3, v_blk0, bias, *, nh):
    b, t, _ = qn3.shape
    bq = min(256, t)
    bk = min(128, t)
    ratio = bq // bk
    assert t % bq == 0 and bq % bk == 0
    scale = LANES ** -0.5

    def kern(bias_ref, q_ref, k_ref, v_ref, o_ref):
        h = pl.program_id(1)
        qi = pl.program_id(2)
        bias_h = bias_ref[h]
        q = q_ref[...].astype(BF16)
        u = (_iota2((bk, bk), 0) > _iota2((bk, bk), 1)).astype(BF16)
        qpos = qi * bq + _iota2((bq, bk), 0)
        kcol = _iota2((bq, bk), 1)
        nblk = (qi + 1) * ratio

        def body(i, carry):
            r_run, acc = carry
            k0 = pl.multiple_of((nblk - 1 - i) * bk, bk)
            k = k_ref[pl.ds(k0, bk), :].astype(BF16)
            v = v_ref[pl.ds(k0, bk), :].astype(BF16)
            z = _dot_nt(q, k) * scale + bias_h
            mask = (k0 + kcol) < qpos
            a, rs = _sb_weights(z, mask, r_run, u)
            return r_run + rs, acc + _dot(a.astype(BF16), v)

        _, acc = lax.fori_loop(0, nblk, body, (jnp.zeros((bq, 1), F32), jnp.zeros((bq, LANES), F32)))
        o_ref[...] = acc.astype(o_ref.dtype)

    return pl.pallas_call(
        kern,
        grid=(b, nh, t // bq),
        in_specs=[pl.BlockSpec(memory_space=pltpu.SMEM),
                  pl.BlockSpec((None, bq, LANES), lambda i, j, qq: (i, qq, j)),
                  pl.BlockSpec((None, t, LANES), lambda i, j, qq: (i, 0, j)),
                  pl.BlockSpec((None, t, LANES), lambda i, j, qq: (i, 0, v_blk0 + j))],
        out_specs=pl.BlockSpec((None, bq, LANES), lambda i, j, qq: (i, qq, j)),
        out_shape=jax.ShapeDtypeStruct((b, t, nh * LANES), _act_dtype(t)),
        compiler_params=_cparams("parallel", "parallel", "arbitrary"),
        name="sb_prompt",
    )(bias, qn3, kn3, proj3)


def _sb_paged(qn3, kn3, proj3, v_blk0, bias, cache_k4, cache_v4, layer, page_table, *, nh):
    b, t, width = qn3.shape
    npages = page_table.shape[1]
    page = cache_k4.shape[2]
    ht = nh * t
    scale = LANES ** -0.5
    bias_rows = jnp.broadcast_to(jnp.repeat(bias, t)[:, None], (ht, page)).astype(F32)
    assert v_blk0 % nh == 0 and page >= t

    def kern(pt_ref, q_ref, kn_ref, vn_ref, bias_ref, ck_ref, cv_ref, o_ref, acc_ref, r_ref, qb_ref):
        p = pl.program_id(1)
        u = (_iota2((page, page), 0) > _iota2((page, page), 1)).astype(BF16)

        def process(k, v, mask):
            z = _dot_nt(qb_ref[...], k.astype(BF16)) * scale + bias_ref[...]
            a, rs = _sb_weights(z, mask, r_ref[...], u)
            acc_ref[...] += _dot(a.astype(BF16), v.astype(BF16))
            r_ref[...] += rs

        @pl.when(p == 0)
        def _():
            blockmask = (_iota2((ht, width), 0) // t) == (_iota2((ht, width), 1) // LANES)
            qt = jnp.concatenate([q_ref[...]] * nh, axis=0)
            qb_ref[...] = jnp.where(blockmask, qt, 0.0).astype(BF16)
            acc_ref[...] = jnp.zeros_like(acc_ref)
            r_ref[...] = jnp.zeros_like(r_ref)
            pad = jnp.zeros((page - t, width), F32)
            kpad = jnp.concatenate([kn_ref[...], pad], axis=0)
            vpad = jnp.concatenate([vn_ref[...], pad], axis=0)
            mask = _iota2((ht, page), 1) < (_iota2((ht, page), 0) % t)
            process(kpad, vpad, mask)

        process(ck_ref[...], cv_ref[...], None)

        @pl.when(p == npages - 1)
        def _():
            blockmask = (_iota2((ht, width), 0) // t) == (_iota2((ht, width), 1) // LANES)
            accm = jnp.where(blockmask, acc_ref[...], 0.0)
            out = accm[0:t, :]
            for hh in range(1, nh):
                out = out + accm[hh * t:(hh + 1) * t, :]
            o_ref[...] = out.astype(o_ref.dtype)

    cache_spec = pl.BlockSpec((None, None, page, width),
                              lambda i, p, pt: (layer, pt[i, npages - 1 - p], 0, 0))
    grid_spec = pltpu.PrefetchScalarGridSpec(
        num_scalar_prefetch=1,
        grid=(b, npages),
        in_specs=[pl.BlockSpec((None, t, width), lambda i, p, pt: (i, 0, 0)),
                  pl.BlockSpec((None, t, width), lambda i, p, pt: (i, 0, 0)),
                  pl.BlockSpec((None, t, width), lambda i, p, pt: (i, 0, v_blk0 // nh)),
                  pl.BlockSpec((ht, page), lambda i, p, pt: (0, 0)),
                  cache_spec, cache_spec],
        out_specs=pl.BlockSpec((None, t, width), lambda i, p, pt: (i, 0, 0)),
        scratch_shapes=[pltpu.VMEM((ht, width), F32), pltpu.VMEM((ht, 1), F32), pltpu.VMEM((ht, width), BF16)],
    )
    return pl.pallas_call(
        kern,
        grid_spec=grid_spec,
        out_shape=jax.ShapeDtypeStruct((b, t, width), _act_dtype(t)),
        compiler_params=_cparams("parallel", "arbitrary"),
        name="sb_paged",
    )(page_table, qn3, kn3, proj3, bias_rows, cache_k4, cache_v4)


def _ffn_act(u3, buf, conv_w):
    b, t, two_f = u3.shape
    f = two_f // 2
    kc = conv_w.shape[0]
    tc = _pick(f, 512)
    nfb = f // tc
    off = 8 - (kc - 1)
    odt = _act_dtype(t)

    def kern(a_ref, b_ref, buf_ref, w_ref, hid_ref, nb_ref, xs):
        xs[off:8, :] = buf_ref[...]
        xs[8:8 + t, :] = a_ref[...]
        nb_ref[...] = xs[t + off:t + 8, :]
        y = xs[off:off + t, :] * w_ref[0:1, :]
        for kk in range(1, kc):
            y = y + xs[off + kk:off + kk + t, :] * w_ref[kk:kk + 1, :]
        hid_ref[...] = (_gelu_tanh(y) * b_ref[...]).astype(hid_ref.dtype)

    return pl.pallas_call(
        kern,
        grid=(b, nfb),
        in_specs=[pl.BlockSpec((None, t, tc), lambda i, j: (i, 0, j)),
                  pl.BlockSpec((None, t, tc), lambda i, j: (i, 0, nfb + j)),
                  pl.BlockSpec((None, kc - 1, tc), lambda i, j: (i, 0, j)),
                  pl.BlockSpec((kc, tc), lambda i, j: (0, j))],
        out_specs=[pl.BlockSpec((None, t, tc), lambda i, j: (i, 0, j)),
                   pl.BlockSpec((None, kc - 1, tc), lambda i, j: (i, 0, j))],
        out_shape=[jax.ShapeDtypeStruct((b, t, f), odt), jax.ShapeDtypeStruct((b, kc - 1, f), F32)],
        scratch_shapes=[pltpu.VMEM((t + 8, tc), F32)],
        compiler_params=_cparams("parallel", "parallel"),
        name="ffn_act",
    )(u3, u3, buf, conv_w)


def _pad_cols(w, n):
    return jnp.pad(w, ((0, 0), (0, n - w.shape[1])))


def _rot_tables(pos, dk):
    half = dk // 2
    inv = 1.0 / (RET_THETA ** (jnp.arange(half, dtype=F32) / half))
    ang = pos.astype(F32)[:, None] * inv[None, :]
    cos, sin = jnp.cos(ang), jnp.sin(ang)
    return jnp.concatenate([cos, cos], -1), jnp.concatenate([-sin, sin], -1)


def _trunk(x3, p4, pos, conv_a, rec_a, rec_b, rec_d, ffn_buf, W, dims, paged=None):
    b, t, d = x3.shape
    m = b * t
    (h_a, h_b, dv_b, h_c, h_d, dv_d, r_d) = dims
    qa = h_a * LANES
    conv_ch = 3 * qa
    qb = h_b * LANES
    vb = h_b * dv_b
    qc = h_c * LANES
    qd = h_d * LANES
    vd = h_d * dv_d
    depth = W["norm_mix"].shape[0]
    h2 = x3.reshape(m, d)
    cos_t, sin_t = _rot_tables(pos, LANES)
    log_gamma = jnp.log1p(-jnp.exp2(-5.0 - jnp.arange(h_b, dtype=F32)))
    ld_ret = jnp.broadcast_to(jnp.repeat(log_gamma, LANES)[None, None, :], (b, t, qb))
    conv_l, ra_l, rb_l, k_l, v_l, rd_l, ffn_l = [], [], [], [], [], [], []

    for i in range(depth):
        j = i // 2
        xn = _rms_cast(h2, W["norm_mix"][i])
        if i % 2 == 0:
            proj3 = _matmul(xn, W["w_in_even_main"][j]).reshape(b, t, -1)
            sp3 = _matmul(xn, W["w_in_even_small"][j], tn_pref=LANES).reshape(b, t, LANES)
            qkv3, c_new = _conv_a_prep(proj3, conv_a[j], W["conv_a_w"][j], qa=qa, dk=LANES)
            o_a, sa_new = _gated_delta(qkv3, sp3, proj3, conv_ch // LANES, W["alog_p"][j], W["dtb_p"][j],
                                       W["a_out_norm"][j].reshape(1, LANES), rec_a[j], nh=h_a)
            off_qb = conv_ch + qa
            o_b, sb_new = _gla(proj3, off_qb // LANES, (off_qb + qb) // LANES, (off_qb + 2 * qb) // dv_b,
                               (off_qb + 2 * qb + vb) // dv_b, W["b_gn"][j].reshape(h_b, 1, dv_b), rec_b[j],
                               nh=h_b, dv=dv_b, k_scale=LANES ** -0.5, norm="group",
                               rot=(cos_t, sin_t), ld3=ld_ret)
            mix = jnp.concatenate([o_a, o_b], axis=-1).astype(BF16).reshape(m, -1)
            h2 = _matmul(mix, W["w_out_even"][j], res=h2)
            conv_l.append(c_new)
            ra_l.append(sa_new)
            rb_l.append(sb_new)
        else:
            proj3 = _matmul(xn, W["w_in_odd_main"][j]).reshape(b, t, -1)
            sp3 = _matmul(xn, W["w_in_odd_small"][j], tn_pref=LANES).reshape(b, t, LANES)
            qn3 = _head_rms(proj3, 0, W["c_q_norm"][j], nh=h_c)
            kn3 = _head_rms(proj3, h_c, W["c_k_norm"][j], nh=h_c)
            if paged is None:
                o_c = _sb_prompt(qn3, kn3, proj3, 2 * h_c, W["c_logit_bias"][j], nh=h_c)
            else:
                cache_k4, cache_v4, page_table = paged
                o_c = _sb_paged(qn3, kn3, proj3, 2 * h_c, W["c_logit_bias"][j], cache_k4, cache_v4, j,
                                page_table, nh=h_c)
            off_qd = 3 * qc
            o_d, sd_new = _gla(proj3, off_qd // LANES, (off_qd + qd) // LANES, (off_qd + 2 * qd) // dv_d,
                               (off_qd + 2 * qd + vd) // dv_d, W["d_gain"][j], rec_d[j],
                               nh=h_d, dv=dv_d, k_scale=LANES ** -0.5, norm="rms",
                               lowrank=(sp3, W["d_w2p"][j], W["d_alpha_b"][j].reshape(1, qd)))
            mix = jnp.concatenate([o_c, o_d], axis=-1).astype(BF16).reshape(m, -1)
            h2 = _matmul(mix, W["w_out_odd"][j], res=h2)
            k_l.append(kn3.reshape(b, t, h_c, LANES))
            v_l.append(proj3[:, :, 2 * qc:3 * qc].reshape(b, t, h_c, LANES))
            rd_l.append(sd_new)
        xn = _rms_cast(h2, W["norm_ffn"][i])
        u3 = _matmul(xn, W["w_ffn_in"][i]).reshape(b, t, -1)
        hid, fb = _ffn_act(u3, ffn_buf[i], W["ffn_conv_w"][i])
        f = hid.shape[-1]
        tk_f = f // 2 if (f // 2) % LANES == 0 else f
        h2 = _matmul(hid.astype(BF16).reshape(m, f), W["w_ffn_out"][i], res=h2, tm_pref=512, tk=tk_f)
        ffn_l.append(fb)
        xn = _rms_cast(h2, W["ple_norm"][i])
        h2 = _matmul(xn, W["w_ple_gate"][i], ple=(h2, p4[i].reshape(m, -1).astype(BF16), W["w_ple_in"][i]))
    return (h2.reshape(b, t, d), jnp.stack(conv_l), jnp.stack(ra_l), jnp.stack(rb_l), jnp.stack(k_l),
            jnp.stack(v_l), jnp.stack(rd_l), jnp.stack(ffn_l))


def kernel(x_prompt, x_sample, state_a_conv, state_a_rec, state_b_rec, cache_k, cache_v, state_d_rec, state_ffn_conv, page_table, p_prompt, p_sample, norm_mix, w_in_even, conv_a_w, a_log, dt_bias, a_out_norm, b_gn, w_out_even, w_in_odd, c_q_norm, c_k_norm, c_logit_bias, d_alpha_w2, d_alpha_b, d_out_norm, w_out_odd, norm_ffn, w_ffn_in, ffn_conv_w, w_ffn_out, ple_norm, w_ple_gate, w_ple_in):
    n_even, h_a = a_log.shape
    n_odd, h_c = c_logit_bias.shape
    h_b, dv_b = state_b_rec.shape[2], state_b_rec.shape[4]
    h_d, dv_d = state_d_rec.shape[2], state_d_rec.shape[4]
    r_d = d_alpha_w2.shape[1]
    depth = norm_mix.shape[0]
    conv_ch = conv_a_w.shape[2]
    qa = h_a * LANES
    qc = h_c * LANES
    qd = h_d * LANES
    assert conv_ch == 3 * qa and state_a_rec.shape[3:] == (LANES, LANES)
    assert state_b_rec.shape[3] == LANES and state_d_rec.shape[3] == LANES and cache_k.shape[4] == LANES
    assert 2 * h_a <= LANES and r_d <= LANES
    dims = (h_a, h_b, dv_b, h_c, h_d, dv_d, r_d)

    w_main_e = jnp.concatenate([w_in_even[:, :, :conv_ch], w_in_even[:, :, conv_ch + 2 * h_a:]], axis=-1).astype(BF16)
    w_small_e = jnp.pad(w_in_even[:, :, conv_ch:conv_ch + 2 * h_a],
                        ((0, 0), (0, 0), (0, LANES - 2 * h_a))).astype(BF16)
    n_main_o = 3 * qc + 2 * qd + 2 * h_d * dv_d
    w_main_o = w_in_odd[:, :, :n_main_o].astype(BF16)
    w_small_o = jnp.pad(w_in_odd[:, :, n_main_o:], ((0, 0), (0, 0), (0, LANES - r_d))).astype(BF16)
    pad_ab = lambda v: jnp.pad(v, ((0, 0), (h_a, LANES - 2 * h_a)))[:, None, :]
    W = {
        "norm_mix": norm_mix, "norm_ffn": norm_ffn, "ple_norm": ple_norm,
        "w_in_even_main": w_main_e, "w_in_even_small": w_small_e,
        "w_in_odd_main": w_main_o, "w_in_odd_small": w_small_o,
        "conv_a_w": conv_a_w, "alog_p": pad_ab(a_log), "dtb_p": pad_ab(dt_bias),
        "a_out_norm": a_out_norm, "b_gn": b_gn,
        "w_out_even": w_out_even.astype(BF16), "w_out_odd": w_out_odd.astype(BF16),
        "c_q_norm": c_q_norm, "c_k_norm": c_k_norm, "c_logit_bias": c_logit_bias,
        "d_w2p": jnp.pad(d_alpha_w2, ((0, 0), (0, LANES - r_d), (0, 0))), "d_alpha_b": d_alpha_b,
        "d_gain": jnp.broadcast_to(d_out_norm[:, None, None, :], (n_odd, h_d, 1, dv_d)),
        "w_ffn_in": w_ffn_in.astype(BF16), "ffn_conv_w": ffn_conv_w, "w_ffn_out": w_ffn_out.astype(BF16),
        "w_ple_gate": w_ple_gate.astype(BF16), "w_ple_in": w_ple_in.astype(BF16),
    }

    bp, tp = x_prompt.shape[0], x_prompt.shape[1]
    kconv = conv_a_w.shape[1]
    kffn = ffn_conv_w.shape[1]
    f = w_ffn_out.shape[1]
    zeros = lambda *s: jnp.zeros(s, F32)
    out_p = _trunk(x_prompt, p_prompt, jnp.arange(tp, dtype=jnp.int32),
                   zeros(n_even, bp, kconv - 1, conv_ch), zeros(n_even, bp, h_a, LANES, LANES),
                   zeros(n_even, bp, h_b, LANES, dv_b), zeros(n_odd, bp, h_d, LANES, dv_d),
                   zeros(depth, bp, kffn - 1, f), W, dims)

    ts = x_sample.shape[1]
    n_pool, page = cache_k.shape[1], cache_k.shape[2]
    past_len = page_table.shape[1] * page
    pos_s = past_len + jnp.arange(ts, dtype=jnp.int32)
    cache_k4 = cache_k.reshape(n_odd, n_pool, page, qc)
    cache_v4 = cache_v.reshape(n_odd, n_pool, page, qc)
    out_s = _trunk(x_sample, p_sample, pos_s, state_a_conv, state_a_rec, state_b_rec, state_d_rec,
                   state_ffn_conv, W, dims, paged=(cache_k4, cache_v4, page_table))

    res = []
    for a, s in zip(out_p, out_s):
        res += [a, s]
    return tuple(res)
```

```python
import math

import jax
import jax.numpy as jnp
from jax import lax
from jax.experimental import pallas as pl
from jax.experimental.pallas import tpu as pltpu

F32 = jnp.float32
BF16 = jnp.bfloat16
EPS = 1e-6
LANES = 128
BF16_SUBLANES = 16
CHUNK = 64
SUB = 16
RET_THETA = 10000.0
TAU_D = 16.0
VMEM_LIMIT = 56 * 1024 * 1024


def _cparams(*sem):
    return pltpu.CompilerParams(dimension_semantics=sem, vmem_limit_bytes=VMEM_LIMIT)


def _pick(n, pref, mult=LANES):
    t = (min(pref, n) // mult) * mult
    while t >= mult:
        if n % t == 0:
            return t
        t -= mult
    return n


def _act_dtype(t):
    return BF16 if t % BF16_SUBLANES == 0 else F32


def _dot(a, b):
    return jnp.dot(a, b, preferred_element_type=F32)


def _dot_nt(a, b):
    return lax.dot_general(a, b, (((1,), (1,)), ((), ())), preferred_element_type=F32)


def _dot_tn(a, b):
    return lax.dot_general(a, b, (((0,), (0,)), ((), ())), preferred_element_type=F32)


def _bf(x):
    return x.astype(BF16)


def _silu(x):
    return x * jax.nn.sigmoid(x)


def _softplus(x):
    return jnp.maximum(x, 0.0) + jnp.log1p(jnp.exp(-jnp.abs(x)))


def _log_sigmoid(x):
    return jnp.minimum(x, 0.0) - jnp.log1p(jnp.exp(-jnp.abs(x)))


def _gelu_tanh(x):
    c = math.sqrt(2.0 / math.pi)
    return 0.5 * x * (1.0 + jnp.tanh(c * (x + 0.044715 * (x * x * x))))


def _iota2(shape, dim):
    return lax.broadcasted_iota(jnp.int32, shape, dim)


def _split2(x):
    hi = x.astype(BF16)
    return hi, (x - hi.astype(F32)).astype(BF16)


def _split3(x):
    hi = x.astype(BF16)
    r1 = x - hi.astype(F32)
    mid = r1.astype(BF16)
    lo = (r1 - mid.astype(F32)).astype(BF16)
    return hi, mid, lo


def _dot01(m01, x):
    hi, mid, lo = _split3(x)
    return _dot(m01, hi) + _dot(m01, mid) + _dot(m01, lo)


def _rms_cast(x2, g):
    m, d = x2.shape
    tm = _pick(m, 256, 8)

    def kern(x_ref, g_ref, o_ref):
        x = x_ref[...]
        r = lax.rsqrt(jnp.mean(x * x, axis=-1, keepdims=True) + EPS)
        o_ref[...] = (x * r * g_ref[...]).astype(o_ref.dtype)

    return pl.pallas_call(
        kern,
        grid=(m // tm,),
        in_specs=[pl.BlockSpec((tm, d), lambda i: (i, 0)), pl.BlockSpec((1, d), lambda i: (0, 0))],
        out_specs=pl.BlockSpec((tm, d), lambda i: (i, 0)),
        out_shape=jax.ShapeDtypeStruct((m, d), BF16),
        compiler_params=_cparams("parallel"),
        name="rms_cast",
    )(x2, g.reshape(1, d))


def _matmul(x, w, *, res=None, ple=None, tm_pref=1024, tn_pref=512, tk=None):
    m, kdim = x.shape
    n = w.shape[1]
    tm = _pick(m, tm_pref, 8)
    tn = _pick(n, tn_pref)
    tk = kdim if tk is None else tk
    nk = kdim // tk
    assert kdim % tk == 0 and m % tm == 0 and n % tn == 0

    in_specs = [pl.BlockSpec((tm, tk), lambda i, j, k: (i, k)), pl.BlockSpec((tk, tn), lambda i, j, k: (k, j))]
    args = [x, w]
    if res is not None:
        in_specs.append(pl.BlockSpec((tm, tn), lambda i, j, k: (i, j)))
        args.append(res)
    if ple is not None:
        h, p, w_in = ple
        pdim = p.shape[1]
        in_specs += [pl.BlockSpec((tm, tn), lambda i, j, k: (i, j)),
                     pl.BlockSpec((tm, pdim), lambda i, j, k: (i, 0)),
                     pl.BlockSpec((pdim, tn), lambda i, j, k: (0, j))]
        args += [h, p, w_in]

    def kern(*refs):
        x_ref, w_ref = refs[0], refs[1]
        o_ref = refs[len(args)]
        acc_ref = refs[len(args) + 1] if nk > 1 else None

        def epilogue(acc):
            if res is not None:
                return refs[2][...] + acc
            if ple is not None:
                h_ref, p_ref, wi_ref = refs[2], refs[3], refs[4]
                emb = _dot(p_ref[...], wi_ref[...])
                return h_ref[...] + jax.nn.sigmoid(acc) * emb
            return acc

        part = _dot(x_ref[...], w_ref[...])
        if nk == 1:
            o_ref[...] = epilogue(part)
        else:
            k = pl.program_id(2)

            @pl.when(k == 0)
            def _():
                acc_ref[...] = part

            @pl.when(jnp.logical_and(k > 0, k < nk - 1))
            def _():
                acc_ref[...] += part

            @pl.when(k == nk - 1)
            def _():
                o_ref[...] = epilogue(acc_ref[...] + part)

    return pl.pallas_call(
        kern,
        grid=(m // tm, n // tn, nk),
        in_specs=in_specs,
        out_specs=pl.BlockSpec((tm, tn), lambda i, j, k: (i, j)),
        out_shape=jax.ShapeDtypeStruct((m, n), F32),
        scratch_shapes=[pltpu.VMEM((tm, tn), F32)] if nk > 1 else [],
        compiler_params=_cparams("parallel", "parallel", "arbitrary"),
        name="matmul",
    )(*args)


def _load_chunk(ref, r0, t):
    if t >= CHUNK:
        return ref[pl.ds(r0, CHUNK), :]
    x = ref[...]
    return jnp.concatenate([x, jnp.zeros((CHUNK - t, x.shape[1]), x.dtype)], axis=0)


def _store_chunk(ref, r0, t, lanes, val):
    if t >= CHUNK:
        ref[pl.ds(r0, CHUNK), lanes] = val.astype(ref.dtype)
    else:
        ref[:, lanes] = val[:t].astype(ref.dtype)


def _time_tile(t):
    tt = min(t, 512)
    assert t % tt == 0 and (tt % CHUNK == 0 or t < CHUNK)
    return tt


def _inv_unit_lower(a, eye, same_sub, nb):
    a_d = [jnp.where(same_sub, x, 0.0) for x in a]
    low = [x - y for x, y in zip(a, a_d)]
    dinv = [eye - x for x in a_d]
    pb = [_bf(-x) for x in a_d]
    s = 2
    while s < SUB:
        pb = [_bf(_dot(x, x)) for x in pb]
        dinv = [d + _dot(_bf(d), x) for d, x in zip(dinv, pb)]
        s *= 2
    if nb == 1:
        return dinv
    db = [_bf(d) for d in dinv]
    nmat = [_dot(d, _bf(x)) for d, x in zip(db, low)]
    nb16 = [_bf(x) for x in nmat]
    acc = [eye - x for x in nmat]
    pw = nb16
    sign = -1.0
    for _ in range(2, nb):
        pwm = [_dot(x, y) for x, y in zip(pw, nb16)]
        pw = [_bf(x) for x in pwm]
        sign = -sign
        acc = [x + sign * y for x, y in zip(acc, pwm)]
    return [_dot(_bf(x), d) for x, d in zip(acc, db)]


def _conv_a_prep(proj3, buf, conv_w, *, qa, dk):
    b, t, _ = proj3.shape
    kc, ch = conv_w.shape
    tc = _pick(qa, 512)
    nqb = qa // tc
    assert ch % tc == 0
    off = 8 - (kc - 1)

    def kern(x_ref, buf_ref, w_ref, o_ref, cn_ref, xs):
        j = pl.program_id(1)
        xs[off:8, :] = buf_ref[...]
        xs[8:8 + t, :] = x_ref[...]
        cn_ref[...] = xs[t + off:t + 8, :]
        is_q = j < nqb
        is_k = j < 2 * nqb
        scale = jnp.where(is_q, dk ** -0.5, 1.0).astype(F32)
        for hh in range(tc // LANES):
            sl = slice(hh * LANES, (hh + 1) * LANES)
            y = xs[off:off + t, sl] * w_ref[0:1, sl]
            for kk in range(1, kc):
                y = y + xs[off + kk:off + kk + t, sl] * w_ref[kk:kk + 1, sl]
            s = _silu(y)
            nrm = s * lax.rsqrt(jnp.sum(s * s, axis=-1, keepdims=True) + EPS) * scale
            o_ref[:, sl] = jnp.where(is_k, nrm, s)

    return pl.pallas_call(
        kern,
        grid=(b, ch // tc),
        in_specs=[pl.BlockSpec((None, t, tc), lambda i, j: (i, 0, j)),
                  pl.BlockSpec((None, kc - 1, tc), lambda i, j: (i, 0, j)),
                  pl.BlockSpec((kc, tc), lambda i, j: (0, j))],
        out_specs=[pl.BlockSpec((None, t, tc), lambda i, j: (i, 0, j)),
                   pl.BlockSpec((None, kc - 1, tc), lambda i, j: (i, 0, j))],
        out_shape=[jax.ShapeDtypeStruct((b, t, ch), F32), jax.ShapeDtypeStruct((b, kc - 1, ch), F32)],
        scratch_shapes=[pltpu.VMEM((t + 8, tc), F32)],
        compiler_params=_cparams("parallel", "parallel"),
        name="conv_a_prep",
    )(proj3, buf, conv_w)


def _gated_delta(qkv3, sp3, proj3, gate_blk0, alog_p, dtb_p, a_norm, s0, *, nh):
    b, t, _ = qkv3.shape
    c = CHUNK
    hs = min(8, nh)
    gh = min(2, hs)
    gs = hs // gh
    n = gh * c
    tt = _time_tile(t)
    nchunks = max(tt // c, 1)
    wl = hs * LANES
    assert nh % hs == 0 and hs % gh == 0 and gate_blk0 % hs == 0
    odt = _act_dtype(t)

    def kern(q_ref, k_ref, v_ref, sp_ref, gate_ref, alog_ref, dtb_ref, an_ref, s0_ref, o_ref, s_ref):
        hb = pl.program_id(1)
        ti = pl.program_id(2)

        @pl.when(ti == 0)
        def _():
            s_ref[...] = s0_ref[...]

        row = _iota2((n, n), 0)
        col = _iota2((n, n), 1)
        same = (row // c) == (col // c)
        incl = jnp.logical_and(same, col <= row)
        strict = jnp.logical_and(same, col < row)
        same_sub = (row // SUB) == (col // SUB)
        lbd = incl.astype(BF16)
        eye = (row == col).astype(F32)
        lane = _iota2((n, LANES), 1)
        rown = _iota2((n, 1), 0)
        valid = (rown % c) < t
        neg_a = -jnp.exp(alog_ref[...])
        dtb = dtb_ref[...]
        an = an_ref[...]

        def stack(x, g0):
            return jnp.concatenate([x[:, (g0 + i) * LANES:(g0 + i + 1) * LANES] for i in range(gh)], axis=0)

        def chunk(ci, carry):
            r0 = pl.multiple_of(ci * c, c)
            qc = _load_chunk(q_ref, r0, tt)
            kc = _load_chunk(k_ref, r0, tt)
            vc = _load_chunk(v_ref, r0, tt)
            gc = _load_chunk(gate_ref, r0, tt)
            sp = _load_chunk(sp_ref, r0, tt)
            beta_all = jax.nn.sigmoid(sp)
            g_all = neg_a * _softplus(sp + dtb)
            bt = jnp.concatenate([beta_all] * gh, axis=0)
            gt = jnp.concatenate([g_all] * gh, axis=0)
            grp = range(gs)
            rowhead = [hb * hs + gi * gh + rown // c for gi in grp]
            q = [stack(qc, gi * gh) for gi in grp]
            k = [stack(kc, gi * gh) for gi in grp]
            v = [stack(vc, gi * gh) for gi in grp]
            s_old = [[s_ref[gi * gh + i] for i in range(gh)] for gi in grp]
            bcol = [jnp.sum(jnp.where(jnp.logical_and(lane == rh, valid), bt, 0.0), axis=1, keepdims=True)
                    for rh in rowhead]
            g_m = [jnp.where(jnp.logical_and(lane == nh + rh, valid), gt, 0.0) for rh in rowhead]
            gcol = [jnp.sum(_dot01(lbd, x), axis=1, keepdims=True) for x in g_m]
            grow = [jnp.sum(eye * x, axis=0, keepdims=True) for x in gcol]
            decay = [jnp.where(incl, jnp.exp(jnp.where(incl, x - y, 0.0)), 0.0) for x, y in zip(gcol, grow)]
            eg = [jnp.exp(x) for x in gcol]
            kb = [_bf(x) for x in k]
            kk = [_dot_nt(x, x) for x in kb]
            amat = [jnp.where(strict, x * d * bc, 0.0) for x, d, bc in zip(kk, decay, bcol)]
            inv = _inv_unit_lower(amat, eye, same_sub, c // SUB)
            rhs = [jnp.concatenate([vv * bc, kx * (bc * e)], axis=1)
                   for vv, kx, bc, e in zip(v, k, bcol, eg)]
            sol = [_dot(_bf(x), _bf(y)) for x, y in zip(inv, rhs)]
            qe = [x * e for x, e in zip(q, eg)]
            sob = [[_bf(x) for x in row_] for row_ in s_old]
            u = [jnp.concatenate([sol[gi][i * c:(i + 1) * c, :LANES]
                                  - _dot(_bf(sol[gi][i * c:(i + 1) * c, LANES:]), sob[gi][i])
                                  for i in range(gh)], axis=0) for gi in grp]
            oi = [jnp.concatenate([_dot(_bf(qe[gi][i * c:(i + 1) * c]), sob[gi][i]) for i in range(gh)], axis=0)
                  for gi in grp]
            ub = [_bf(x) for x in u]
            attn = [_dot_nt(_bf(x), y) * d for x, y, d in zip(q, kb, decay)]
            o = [x + _dot(_bf(y), z) for x, y, z in zip(oi, attn, ub)]
            on = [x * lax.rsqrt(jnp.mean(x * x, axis=-1, keepdims=True) + EPS) * an for x in o]
            for gi in grp:
                for i in range(gh):
                    rs = slice(i * c, (i + 1) * c)
                    lanes = slice((gi * gh + i) * LANES, (gi * gh + i + 1) * LANES)
                    g_last = gcol[gi][(i + 1) * c - 1:(i + 1) * c, :]
                    k_dec = k[gi][rs] * jnp.exp(g_last - gcol[gi][rs])
                    s_ref[gi * gh + i] = s_old[gi][i] * jnp.exp(g_last) + _dot_tn(_bf(k_dec), ub[gi][rs])
                    _store_chunk(o_ref, r0, tt, lanes, on[gi][rs] * _silu(gc[:, lanes]))
            return carry

        if nchunks == 1:
            chunk(0, 0)
        else:
            lax.fori_loop(0, nchunks, chunk, 0)

    blk = lambda off: pl.BlockSpec((None, tt, wl), lambda i, j, ti: (i, ti, off + j))
    vec = pl.BlockSpec((1, LANES), lambda i, j, ti: (0, 0))
    st = pl.BlockSpec((None, hs, LANES, LANES), lambda i, j, ti: (i, j, 0, 0))
    return pl.pallas_call(
        kern,
        grid=(b, nh // hs, t // tt),
        in_specs=[blk(0), blk(nh // hs), blk(2 * nh // hs),
                  pl.BlockSpec((None, tt, LANES), lambda i, j, ti: (i, ti, 0)),
                  blk(gate_blk0 // hs), vec, vec, vec, st],
        out_specs=[blk(0), st],
        out_shape=[jax.ShapeDtypeStruct((b, t, nh * LANES), odt),
                   jax.ShapeDtypeStruct((b, nh, LANES, LANES), F32)],
        compiler_params=_cparams("parallel", "parallel", "arbitrary"),
        name="gated_delta",
    )(qkv3, qkv3, qkv3, sp3, proj3, alog_p, dtb_p, a_norm, s0)


def _gla(proj3, q_blk0, k_blk0, v_blk0, gate_blk0, gain, s0, *, nh, dv, k_scale, norm,
         rot=None, ld3=None, lowrank=None):
    b, t, _ = proj3.shape
    c = CHUNK
    sb = SUB
    nsb = c // sb
    dk = LANES
    hg = min(4, nh)
    tt = _time_tile(t)
    nchunks = max(tt // c, 1)
    assert nh % hg == 0 and q_blk0 % hg == 0 and k_blk0 % hg == 0 and v_blk0 % hg == 0 and gate_blk0 % hg == 0
    odt = _act_dtype(t)
    n_in = 5 + (2 if rot is not None else 0) + (1 if ld3 is not None else 0) + (3 if lowrank is not None else 0)

    def kern(*refs):
        q_ref, k_ref, v_ref, gate_ref, gain_ref = refs[:5]
        pos = 5
        if rot is not None:
            cos_ref, sin_ref = refs[pos], refs[pos + 1]
            pos += 2
        if ld3 is not None:
            ld_ref = refs[pos]
            pos += 1
        if lowrank is not None:
            sp_ref, w2_ref, ab_ref = refs[pos:pos + 3]
            pos += 3
        s0_ref = refs[pos]
        o_ref, s_ref = refs[n_in + 1], refs[n_in + 2]
        st_ref = refs[n_in + 3]
        ti = pl.program_id(2)

        @pl.when(ti == 0)
        def _():
            for hh in range(hg):
                st_ref[hh] = s0_ref[hh].T

        row = _iota2((c, c), 0)
        col = _iota2((c, c), 1)
        lincl = (col <= row).astype(BF16)
        same = (row // sb) == (col // sb)
        diag_mask = jnp.logical_and(same, col <= row)
        below = (col // sb) < (row // sb)
        colmasks = [jnp.logical_and(same, (col % sb) == j) for j in range(sb)]
        rvalid = _iota2((c, 1), 0) < t

        def heads(q, k, v, ld, gate):
            hd = range(hg)
            g = [_dot01(lincl, x) for x in ld]
            st = [st_ref[hh] for hh in hd]
            o = [_dot_nt(_bf(x * jnp.exp(y)), _bf(s)) for x, y, s in zip(q, g, st)]
            offs = [[jnp.zeros((sb, c), F32)] for _ in hd]
            for a in range(1, nsb):
                gs = [x[a * sb - 1:a * sb, :] for x in g]
                qt = [x[a * sb:(a + 1) * sb, :] * jnp.exp(y[a * sb:(a + 1) * sb, :] - z)
                      for x, y, z in zip(q, g, gs)]
                kt = [x * jnp.exp(jnp.minimum(z - y, 0.0)) for x, y, z in zip(k, g, gs)]
                for hh in hd:
                    offs[hh].append(_dot_nt(_bf(qt[hh]), _bf(kt[hh])))
            s_off = [jnp.concatenate(x, axis=0) for x in offs]
            g3 = [x.reshape(nsb, sb, dk) for x in g]
            q3 = [x.reshape(nsb, sb, dk) for x in q]
            k3 = [x.reshape(nsb, sb, dk) for x in k]
            pd = [jnp.zeros((c, c), F32) for _ in hd]
            for j in range(sb):
                e = [jnp.exp(jnp.minimum(x - x[:, j:j + 1, :], 0.0)) for x in g3]
                colj = [jnp.sum(x * y[:, j:j + 1, :] * z, axis=-1, keepdims=True).reshape(c, 1)
                        for x, y, z in zip(q3, k3, e)]
                pd = [jnp.where(colmasks[j], x, y) for x, y in zip(colj, pd)]
            scores = [jnp.where(diag_mask, x, jnp.where(below, y, 0.0)) for x, y in zip(pd, s_off)]
            vb = [_bf(x) for x in v]
            o = [x + _dot(_bf(y), z) for x, y, z in zip(o, scores, vb)]
            g_last = [x[c - 1:c, :] for x in g]
            k_dec = [x * jnp.exp(y - z) for x, y, z in zip(k, g_last, g)]
            for hh in hd:
                st_ref[hh] = st[hh] * jnp.exp(g_last[hh]) + _dot_tn(vb[hh], _bf(k_dec[hh]))
            if norm == "group":
                oc = [x - jnp.mean(x, axis=-1, keepdims=True) for x in o]
                on = [x * lax.rsqrt(jnp.mean(x * x, axis=-1, keepdims=True) + EPS) for x in oc]
            else:
                on = [x * lax.rsqrt(jnp.mean(x * x, axis=-1, keepdims=True) + EPS) for x in o]
            return [on[hh] * gain_ref[hh] * _silu(gate[hh]) for hh in hd]

        def chunk(ci, carry):
            r0 = pl.multiple_of(ci * c, c)
            qc = _load_chunk(q_ref, r0, tt)
            kc = _load_chunk(k_ref, r0, tt)
            vc = _load_chunk(v_ref, r0, tt)
            gatec = _load_chunk(gate_ref, r0, tt)
            if rot is not None:
                cs = _load_chunk(cos_ref, r0, tt)
                sn = _load_chunk(sin_ref, r0, tt)
            if ld3 is not None:
                ldc = _load_chunk(ld_ref, r0, tt)
            else:
                sp = _load_chunk(sp_ref, r0, tt)
                logit = _dot(_bf(sp), _bf(w2_ref[...])) + ab_ref[...]
                ldc = jnp.where(rvalid, _log_sigmoid(logit) / TAU_D, 0.0)
            ks = [slice(hh * dk, (hh + 1) * dk) for hh in range(hg)]
            vs = [slice(hh * dv, (hh + 1) * dv) for hh in range(hg)]
            q = [qc[:, s] for s in ks]
            k = [kc[:, s] for s in ks]
            if rot is not None:
                q = [x * cs + pltpu.roll(x, dk // 2, axis=1) * sn for x in q]
                k = [x * cs + pltpu.roll(x, dk // 2, axis=1) * sn for x in k]
            k = [x * k_scale for x in k]
            out = heads(q, k, [vc[:, s] for s in vs], [ldc[:, s] for s in ks], [gatec[:, s] for s in vs])
            for hh in range(hg):
                _store_chunk(o_ref, r0, tt, vs[hh], out[hh])
            return carry

        if nchunks == 1:
            chunk(0, 0)
        else:
            lax.fori_loop(0, nchunks, chunk, 0)

        @pl.when(ti == pl.num_programs(2) - 1)
        def _():
            for hh in range(hg):
                s_ref[hh] = st_ref[hh].T

    def blk(off, width):
        return pl.BlockSpec((None, tt, hg * width), lambda i, j, ti: (i, ti, off // hg + j))

    in_specs = [blk(q_blk0, dk), blk(k_blk0, dk), blk(v_blk0, dv), blk(gate_blk0, dv),
                pl.BlockSpec((hg, 1, dv), lambda i, j, ti: (j, 0, 0))]
    args = [proj3, proj3, proj3, proj3, gain]
    if rot is not None:
        in_specs += [pl.BlockSpec((tt, dk), lambda i, j, ti: (ti, 0))] * 2
        args += list(rot)
    if ld3 is not None:
        in_specs.append(blk(0, dk))
        args.append(ld3)
    if lowrank is not None:
        sp3, w2p, ab = lowrank
        in_specs += [pl.BlockSpec((None, tt, LANES), lambda i, j, ti: (i, ti, 0)),
                     pl.BlockSpec((LANES, hg * dk), lambda i, j, ti: (0, j)),
                     pl.BlockSpec((1, hg * dk), lambda i, j, ti: (0, j))]
        args += [sp3, w2p, ab]
    st_spec = pl.BlockSpec((None, hg, dk, dv), lambda i, j, ti: (i, j, 0, 0))
    in_specs.append(st_spec)
    args.append(s0)
    assert len(args) == n_in + 1

    return pl.pallas_call(
        kern,
        grid=(b, nh // hg, t // tt),
        in_specs=in_specs,
        out_specs=[blk(0, dv), st_spec],
        out_shape=[jax.ShapeDtypeStruct((b, t, nh * dv), odt),
                   jax.ShapeDtypeStruct((b, nh, dk, dv), F32)],
        scratch_shapes=[pltpu.VMEM((hg, dv, dk), F32)],
        compiler_params=_cparams("parallel", "parallel", "arbitrary"),
        name="gla_" + norm,
    )(*args)


def _head_rms(proj3, blk0, gain, *, nh):
    b, t, _ = proj3.shape
    width = nh * LANES
    tc = _pick(width, 512)
    per = tc // LANES
    c0 = blk0 // per
    assert blk0 % per == 0

    def kern(x_ref, g_ref, o_ref):
        g = g_ref[...]
        for hh in range(per):
            sl = slice(hh * LANES, (hh + 1) * LANES)
            x = x_ref[:, sl]
            o_ref[:, sl] = x * lax.rsqrt(jnp.mean(x * x, axis=-1, keepdims=True) + EPS) * g

    return pl.pallas_call(
        kern,
        grid=(b, width // tc),
        in_specs=[pl.BlockSpec((None, t, tc), lambda i, j: (i, 0, c0 + j)),
                  pl.BlockSpec((1, LANES), lambda i, j: (0, 0))],
        out_specs=pl.BlockSpec((None, t, tc), lambda i, j: (i, 0, j)),
        out_shape=jax.ShapeDtypeStruct((b, t, width), F32),
        compiler_params=_cparams("parallel", "parallel"),
        name="head_rms",
    )(proj3, gain.reshape(1, LANES))


def _sb_weights(z, mask, r_run, u):
    sp = jnp.maximum(z, 0.0) + jnp.log(1.0 + jnp.exp(-jnp.abs(z)))
    l1 = -sp
    if mask is not None:
        l1 = jnp.where(mask, l1, 0.0)
    hi, lo = _split2(l1)
    suf = _dot(hi, u) + _dot(lo, u) + r_run
    a = jnp.exp((z - sp) + suf)
    if mask is not None:
        a = jnp.where(mask, a, 0.0)
    return a, jnp.sum(l1, axis=1, keepdims=True)


def _sb_prompt(qn3, kn3, proj3, v_blk0, bias, *, nh):
    b, t, _ = qn3.shape
    bq = min(256, t)
    bk = min(128, t)
    ratio = bq // bk
    hp = min(2, nh)
    assert t % bq == 0 and bq % bk == 0 and nh % hp == 0 and v_blk0 % hp == 0
    scale = LANES ** -0.5

    def kern(bias_ref, q_ref, k_ref, v_ref, o_ref):
        hb = pl.program_id(1)
        qi = pl.program_id(2)
        u = (_iota2((bk, bk), 0) > _iota2((bk, bk), 1)).astype(BF16)
        dmask = [_iota2((bq, bk), 1) + s * bk < _iota2((bq, bk), 0) for s in range(ratio)]
        qs = [_bf(q_ref[:, hh * LANES:(hh + 1) * LANES]) for hh in range(hp)]
        biases = [bias_ref[hb * hp + hh] for hh in range(hp)]

        def step(kblk, carry, masks):
            pairs = [(hh, s) for hh in range(hp) for s in range(ratio - 1, -1, -1)]
            k0 = {s: pl.multiple_of(kblk * bq + s * bk, bk) for s in range(ratio)}
            lanes = [slice(hh * LANES, (hh + 1) * LANES) for hh in range(hp)]
            kk = [_bf(k_ref[pl.ds(k0[s], bk), lanes[hh]]) for hh, s in pairs]
            vv = [_bf(v_ref[pl.ds(k0[s], bk), lanes[hh]]) for hh, s in pairs]
            z = [_dot_nt(qs[hh], x) * scale + biases[hh] for (hh, s), x in zip(pairs, kk)]
            sp = [jnp.maximum(x, 0.0) + jnp.log(1.0 + jnp.exp(-jnp.abs(x))) for x in z]
            l1 = [-x if masks is None else jnp.where(masks[s], -x, 0.0) for (hh, s), x in zip(pairs, sp)]
            rsum = [jnp.sum(x, axis=1, keepdims=True) for x in l1]
            parts = [_split2(x) for x in l1]
            suf = [_dot(hi, u) + _dot(lo, u) for hi, lo in parts]
            new = []
            for hh in range(hp):
                r_run, acc = carry[hh]
                for idx, (h2, s) in enumerate(pairs):
                    if h2 != hh:
                        continue
                    a = jnp.exp((z[idx] - sp[idx]) + (suf[idx] + r_run))
                    if masks is not None:
                        a = jnp.where(masks[s], a, 0.0)
                    acc = acc + _dot(_bf(a), vv[idx])
                    r_run = r_run + rsum[idx]
                new.append((r_run, acc))
            return tuple(new)

        init = tuple((jnp.zeros((bq, 1), F32), jnp.zeros((bq, LANES), F32)) for _ in range(hp))
        carry = step(qi, init, dmask)
        carry = lax.fori_loop(0, qi, lambda i, cr: step(qi - 1 - i, cr, None), carry)
        for hh in range(hp):
            o_ref[:, hh * LANES:(hh + 1) * LANES] = carry[hh][1].astype(o_ref.dtype)

    wl = hp * LANES
    return pl.pallas_call(
        kern,
        grid=(b, nh // hp, t // bq),
        in_specs=[pl.BlockSpec(memory_space=pltpu.SMEM),
                  pl.BlockSpec((None, bq, wl), lambda i, j, qq: (i, qq, j)),
                  pl.BlockSpec((None, t, wl), lambda i, j, qq: (i, 0, j)),
                  pl.BlockSpec((None, t, wl), lambda i, j, qq: (i, 0, v_blk0 // hp + j))],
        out_specs=pl.BlockSpec((None, bq, wl), lambda i, j, qq: (i, qq, j)),
        out_shape=jax.ShapeDtypeStruct((b, t, nh * LANES), _act_dtype(t)),
        compiler_params=_cparams("parallel", "parallel", "arbitrary"),
        name="sb_prompt",
    )(bias, qn3, kn3, proj3)


def _sb_paged(qn3, kn3, proj3, v_blk0, bias, cache_k4, cache_v4, layer, page_table, *, nh):
    b, t, width = qn3.shape
    npages = page_table.shape[1]
    page = cache_k4.shape[2]
    ht = nh * t
    scale = LANES ** -0.5
    bias_rows = jnp.broadcast_to(jnp.repeat(bias, t)[:, None], (ht, page)).astype(F32)
    assert v_blk0 % nh == 0 and page >= t

    def kern(pt_ref, q_ref, kn_ref, vn_ref, bias_ref, ck_ref, cv_ref, o_ref, acc_ref, r_ref, qb_ref):
        p = pl.program_id(1)
        u = (_iota2((page, page), 0) > _iota2((page, page), 1)).astype(BF16)

        def process(k, v, mask):
            z = _dot_nt(qb_ref[...], _bf(k)) * scale + bias_ref[...]
            a, rs = _sb_weights(z, mask, r_ref[...], u)
            acc_ref[...] += _dot(_bf(a), _bf(v))
            r_ref[...] += rs

        @pl.when(p == 0)
        def _():
            blockmask = (_iota2((ht, width), 0) // t) == (_iota2((ht, width), 1) // LANES)
            qt = jnp.concatenate([q_ref[...]] * nh, axis=0)
            qb_ref[...] = jnp.where(blockmask, qt, 0.0).astype(BF16)
            acc_ref[...] = jnp.zeros_like(acc_ref)
            r_ref[...] = jnp.zeros_like(r_ref)
            pad = jnp.zeros((page - t, width), F32)
            kpad = jnp.concatenate([kn_ref[...], pad], axis=0)
            vpad = jnp.concatenate([vn_ref[...], pad], axis=0)
            mask = _iota2((ht, page), 1) < (_iota2((ht, page), 0) % t)
            process(kpad, vpad, mask)

        process(ck_ref[...], cv_ref[...], None)

        @pl.when(p == npages - 1)
        def _():
            blockmask = (_iota2((ht, width), 0) // t) == (_iota2((ht, width), 1) // LANES)
            accm = jnp.where(blockmask, acc_ref[...], 0.0)
            out = accm[0:t, :]
            for hh in range(1, nh):
                out = out + accm[hh * t:(hh + 1) * t, :]
            o_ref[...] = out.astype(o_ref.dtype)

    cache_spec = pl.BlockSpec((None, None, page, width),
                              lambda i, p, pt: (layer, pt[i, npages - 1 - p], 0, 0))
    grid_spec = pltpu.PrefetchScalarGridSpec(
        num_scalar_prefetch=1,
        grid=(b, npages),
        in_specs=[pl.BlockSpec((None, t, width), lambda i, p, pt: (i, 0, 0)),
                  pl.BlockSpec((None, t, width), lambda i, p, pt: (i, 0, 0)),
                  pl.BlockSpec((None, t, width), lambda i, p, pt: (i, 0, v_blk0 // nh)),
                  pl.BlockSpec((ht, page), lambda i, p, pt: (0, 0)),
                  cache_spec, cache_spec],
        out_specs=pl.BlockSpec((None, t, width), lambda i, p, pt: (i, 0, 0)),
        scratch_shapes=[pltpu.VMEM((ht, width), F32), pltpu.VMEM((ht, 1), F32), pltpu.VMEM((ht, width), BF16)],
    )
    return pl.pallas_call(
        kern,
        grid_spec=grid_spec,
        out_shape=jax.ShapeDtypeStruct((b, t, width), _act_dtype(t)),
        compiler_params=_cparams("parallel", "arbitrary"),
        name="sb_paged",
    )(page_table, qn3, kn3, proj3, bias_rows, cache_k4, cache_v4)


def _ffn_up(xn, w, conv_w, buf, *, b, t):
    m, kdim = xn.shape
    f = w.shape[1] // 2
    kc = conv_w.shape[0]
    tm = _pick(t, 1024, 8)
    tn = _pick(f, 512)
    tpb = t // tm
    nfb = f // tn
    off = 8 - (kc - 1)
    assert m == b * t and t % tm == 0

    def kern(x_ref, wa_ref, wb_ref, cw_ref, buf_ref, hid_ref, nb_ref, xs, carry):
        i = pl.program_id(0)
        j = pl.program_id(1)
        x = x_ref[...]
        a = _dot(x, wa_ref[...])
        gate = _dot(x, wb_ref[...])
        first = (i % tpb) == 0

        @pl.when(first)
        def _():
            xs[off:8, :] = buf_ref[...]

        @pl.when(jnp.logical_not(first))
        def _():
            xs[off:8, :] = carry[j, off:8, :]

        xs[8:8 + tm, :] = a
        tail = xs[tm + off:tm + 8, :]
        carry[j, off:8, :] = tail
        nb_ref[...] = tail
        y = xs[off:off + tm, :] * cw_ref[0:1, :]
        for kk in range(1, kc):
            y = y + xs[off + kk:off + kk + tm, :] * cw_ref[kk:kk + 1, :]
        hid_ref[...] = (_gelu_tanh(y) * gate).astype(hid_ref.dtype)

    return pl.pallas_call(
        kern,
        grid=(m // tm, nfb),
        in_specs=[pl.BlockSpec((tm, kdim), lambda i, j: (i, 0)),
                  pl.BlockSpec((kdim, tn), lambda i, j: (0, j)),
                  pl.BlockSpec((kdim, tn), lambda i, j: (0, nfb + j)),
                  pl.BlockSpec((kc, tn), lambda i, j: (0, j)),
                  pl.BlockSpec((None, kc - 1, tn), lambda i, j: (i // tpb, 0, j))],
        out_specs=[pl.BlockSpec((tm, tn), lambda i, j: (i, j)),
                   pl.BlockSpec((None, kc - 1, tn), lambda i, j: (i // tpb, 0, j))],
        out_shape=[jax.ShapeDtypeStruct((m, f), BF16), jax.ShapeDtypeStruct((b, kc - 1, f), F32)],
        scratch_shapes=[pltpu.VMEM((tm + 8, tn), F32), pltpu.VMEM((nfb, 8, tn), F32)],
        compiler_params=_cparams("arbitrary", "arbitrary"),
        name="ffn_up",
    )(xn, w, w, conv_w, buf)


def _ffn_act(u3, buf, conv_w):
    b, t, two_f = u3.shape
    f = two_f // 2
    kc = conv_w.shape[0]
    tc = _pick(f, 8192)
    nfb = f // tc
    off = 8 - (kc - 1)

    def kern(a_ref, b_ref, buf_ref, w_ref, hid_ref, nb_ref, xs):
        xs[off:8, :] = buf_ref[...]
        xs[8:8 + t, :] = a_ref[...]
        nb_ref[...] = xs[t + off:t + 8, :]
        y = xs[off:off + t, :] * w_ref[0:1, :]
        for kk in range(1, kc):
            y = y + xs[off + kk:off + kk + t, :] * w_ref[kk:kk + 1, :]
        hid_ref[...] = _gelu_tanh(y) * b_ref[...]

    return pl.pallas_call(
        kern,
        grid=(b, nfb),
        in_specs=[pl.BlockSpec((None, t, tc), lambda i, j: (i, 0, j)),
                  pl.BlockSpec((None, t, tc), lambda i, j: (i, 0, nfb + j)),
                  pl.BlockSpec((None, kc - 1, tc), lambda i, j: (i, 0, j)),
                  pl.BlockSpec((kc, tc), lambda i, j: (0, j))],
        out_specs=[pl.BlockSpec((None, t, tc), lambda i, j: (i, 0, j)),
                   pl.BlockSpec((None, kc - 1, tc), lambda i, j: (i, 0, j))],
        out_shape=[jax.ShapeDtypeStruct((b, t, f), F32), jax.ShapeDtypeStruct((b, kc - 1, f), F32)],
        scratch_shapes=[pltpu.VMEM((t + 8, tc), F32)],
        compiler_params=_cparams("parallel", "parallel"),
        name="ffn_act",
    )(u3, u3, buf, conv_w)


def _rot_tables(pos, dk):
    half = dk // 2
    inv = 1.0 / (RET_THETA ** (jnp.arange(half, dtype=F32) / half))
    ang = pos.astype(F32)[:, None] * inv[None, :]
    cos, sin = jnp.cos(ang), jnp.sin(ang)
    return jnp.concatenate([cos, cos], -1), jnp.concatenate([-sin, sin], -1)


def _trunk(x3, p4, pos, conv_a, rec_a, rec_b, rec_d, ffn_buf, W, dims, paged=None):
    b, t, d = x3.shape
    m = b * t
    (h_a, h_b, dv_b, h_c, h_d, dv_d, r_d) = dims
    qa = h_a * LANES
    conv_ch = 3 * qa
    qb = h_b * LANES
    vb = h_b * dv_b
    qc = h_c * LANES
    qd = h_d * LANES
    vd = h_d * dv_d
    depth = W["norm_mix"].shape[0]
    h2 = x3.reshape(m, d)
    cos_t, sin_t = _rot_tables(pos, LANES)
    log_gamma = jnp.log1p(-jnp.exp2(-5.0 - jnp.arange(h_b, dtype=F32)))
    ld_ret = jnp.broadcast_to(jnp.repeat(log_gamma, LANES)[None, None, :], (b, t, qb))
    conv_l, ra_l, rb_l, k_l, v_l, rd_l, ffn_l = [], [], [], [], [], [], []

    for i in range(depth):
        j = i // 2
        xn = _rms_cast(h2, W["norm_mix"][i])
        if i % 2 == 0:
            proj3 = _matmul(xn, W["w_in_even_main"][j]).reshape(b, t, -1)
            sp3 = _matmul(xn, W["w_in_even_small"][j], tn_pref=LANES).reshape(b, t, LANES)
            qkv3, c_new = _conv_a_prep(proj3, conv_a[j], W["conv_a_w"][j], qa=qa, dk=LANES)
            o_a, sa_new = _gated_delta(qkv3, sp3, proj3, conv_ch // LANES, W["alog_p"][j], W["dtb_p"][j],
                                       W["a_out_norm"][j].reshape(1, LANES), rec_a[j], nh=h_a)
            off_qb = conv_ch + qa
            o_b, sb_new = _gla(proj3, off_qb // LANES, (off_qb + qb) // LANES, (off_qb + 2 * qb) // dv_b,
                               (off_qb + 2 * qb + vb) // dv_b, W["b_gn"][j].reshape(h_b, 1, dv_b), rec_b[j],
                               nh=h_b, dv=dv_b, k_scale=LANES ** -0.5, norm="group",
                               rot=(cos_t, sin_t), ld3=ld_ret)
            mix = jnp.concatenate([o_a, o_b], axis=-1).astype(BF16).reshape(m, -1)
            h2 = _matmul(mix, W["w_out_even"][j], res=h2)
            conv_l.append(c_new)
            ra_l.append(sa_new)
            rb_l.append(sb_new)
        else:
            proj3 = _matmul(xn, W["w_in_odd_main"][j]).reshape(b, t, -1)
            sp3 = _matmul(xn, W["w_in_odd_small"][j], tn_pref=LANES).reshape(b, t, LANES)
            qn3 = _head_rms(proj3, 0, W["c_q_norm"][j], nh=h_c)
            kn3 = _head_rms(proj3, h_c, W["c_k_norm"][j], nh=h_c)
            if paged is None:
                o_c = _sb_prompt(qn3, kn3, proj3, 2 * h_c, W["c_logit_bias"][j], nh=h_c)
            else:
                cache_k4, cache_v4, page_table = paged
                o_c = _sb_paged(qn3, kn3, proj3, 2 * h_c, W["c_logit_bias"][j], cache_k4, cache_v4, j,
                                page_table, nh=h_c)
            off_qd = 3 * qc
            o_d, sd_new = _gla(proj3, off_qd // LANES, (off_qd + qd) // LANES, (off_qd + 2 * qd) // dv_d,
                               (off_qd + 2 * qd + vd) // dv_d, W["d_gain"][j], rec_d[j],
                               nh=h_d, dv=dv_d, k_scale=LANES ** -0.5, norm="rms",
                               lowrank=(sp3, W["d_w2p"][j], W["d_alpha_b"][j].reshape(1, qd)))
            mix = jnp.concatenate([o_c, o_d], axis=-1).astype(BF16).reshape(m, -1)
            h2 = _matmul(mix, W["w_out_odd"][j], res=h2)
            k_l.append(kn3.reshape(b, t, h_c, LANES))
            v_l.append(proj3[:, :, 2 * qc:3 * qc].reshape(b, t, h_c, LANES))
            rd_l.append(sd_new)
        xn = _rms_cast(h2, W["norm_ffn"][i])
        if t % BF16_SUBLANES == 0:
            hid, fb = _ffn_up(xn, W["w_ffn_in"][i], W["ffn_conv_w"][i], ffn_buf[i], b=b, t=t)
        else:
            u3 = _matmul(xn, W["w_ffn_in"][i]).reshape(b, t, -1)
            hid, fb = _ffn_act(u3, ffn_buf[i], W["ffn_conv_w"][i])
            hid = hid.astype(BF16).reshape(m, -1)
        f = hid.shape[-1]
        tk_f = f // 2 if (f // 2) % LANES == 0 else f
        h2 = _matmul(hid, W["w_ffn_out"][i], res=h2, tk=tk_f)
        ffn_l.append(fb)
        xn = _rms_cast(h2, W["ple_norm"][i])
        h2 = _matmul(xn, W["w_ple_gate"][i], ple=(h2, p4[i].reshape(m, -1).astype(BF16), W["w_ple_in"][i]))
    return (h2.reshape(b, t, d), jnp.stack(conv_l), jnp.stack(ra_l), jnp.stack(rb_l), jnp.stack(k_l),
            jnp.stack(v_l), jnp.stack(rd_l), jnp.stack(ffn_l))


def kernel(x_prompt, x_sample, state_a_conv, state_a_rec, state_b_rec, cache_k, cache_v, state_d_rec, state_ffn_conv, page_table, p_prompt, p_sample, norm_mix, w_in_even, conv_a_w, a_log, dt_bias, a_out_norm, b_gn, w_out_even, w_in_odd, c_q_norm, c_k_norm, c_logit_bias, d_alpha_w2, d_alpha_b, d_out_norm, w_out_odd, norm_ffn, w_ffn_in, ffn_conv_w, w_ffn_out, ple_norm, w_ple_gate, w_ple_in):
    n_even, h_a = a_log.shape
    n_odd, h_c = c_logit_bias.shape
    h_b, dv_b = state_b_rec.shape[2], state_b_rec.shape[4]
    h_d, dv_d = state_d_rec.shape[2], state_d_rec.shape[4]
    r_d = d_alpha_w2.shape[1]
    depth = norm_mix.shape[0]
    conv_ch = conv_a_w.shape[2]
    qa = h_a * LANES
    qc = h_c * LANES
    qd = h_d * LANES
    assert conv_ch == 3 * qa and state_a_rec.shape[3:] == (LANES, LANES)
    assert state_b_rec.shape[3] == LANES and state_d_rec.shape[3] == LANES and cache_k.shape[4] == LANES
    assert 2 * h_a <= LANES and r_d <= LANES
    dims = (h_a, h_b, dv_b, h_c, h_d, dv_d, r_d)

    w_main_e = jnp.concatenate([w_in_even[:, :, :conv_ch], w_in_even[:, :, conv_ch + 2 * h_a:]], axis=-1).astype(BF16)
    w_small_e = jnp.pad(w_in_even[:, :, conv_ch:conv_ch + 2 * h_a],
                        ((0, 0), (0, 0), (0, LANES - 2 * h_a))).astype(BF16)
    n_main_o = 3 * qc + 2 * qd + 2 * h_d * dv_d
    w_main_o = w_in_odd[:, :, :n_main_o].astype(BF16)
    w_small_o = jnp.pad(w_in_odd[:, :, n_main_o:], ((0, 0), (0, 0), (0, LANES - r_d))).astype(BF16)
    pad_ab = lambda v: jnp.pad(v, ((0, 0), (h_a, LANES - 2 * h_a)))[:, None, :]
    W = {
        "norm_mix": norm_mix, "norm_ffn": norm_ffn, "ple_norm": ple_norm,
        "w_in_even_main": w_main_e, "w_in_even_small": w_small_e,
        "w_in_odd_main": w_main_o, "w_in_odd_small": w_small_o,
        "conv_a_w": conv_a_w, "alog_p": pad_ab(a_log), "dtb_p": pad_ab(dt_bias),
        "a_out_norm": a_out_norm, "b_gn": b_gn,
        "w_out_even": w_out_even.astype(BF16), "w_out_odd": w_out_odd.astype(BF16),
        "c_q_norm": c_q_norm, "c_k_norm": c_k_norm, "c_logit_bias": c_logit_bias,
        "d_w2p": jnp.pad(d_alpha_w2, ((0, 0), (0, LANES - r_d), (0, 0))), "d_alpha_b": d_alpha_b,
        "d_gain": jnp.broadcast_to(d_out_norm[:, None, None, :], (n_odd, h_d, 1, dv_d)),
        "w_ffn_in": w_ffn_in.astype(BF16), "ffn_conv_w": ffn_conv_w, "w_ffn_out": w_ffn_out.astype(BF16),
        "w_ple_gate": w_ple_gate.astype(BF16), "w_ple_in": w_ple_in.astype(BF16),
    }

    bp, tp = x_prompt.shape[0], x_prompt.shape[1]
    kconv = conv_a_w.shape[1]
    kffn = ffn_conv_w.shape[1]
    f = w_ffn_out.shape[1]
    zeros = lambda *s: jnp.zeros(s, F32)
    out_p = _trunk(x_prompt, p_prompt, jnp.arange(tp, dtype=jnp.int32),
                   zeros(n_even, bp, kconv - 1, conv_ch), zeros(n_even, bp, h_a, LANES, LANES),
                   zeros(n_even, bp, h_b, LANES, dv_b), zeros(n_odd, bp, h_d, LANES, dv_d),
                   zeros(depth, bp, kffn - 1, f), W, dims)

    ts = x_sample.shape[1]
    n_pool, page = cache_k.shape[1], cache_k.shape[2]
    past_len = page_table.shape[1] * page
    pos_s = past_len + jnp.arange(ts, dtype=jnp.int32)
    cache_k4 = cache_k.reshape(n_odd, n_pool, page, qc)
    cache_v4 = cache_v.reshape(n_odd, n_pool, page, qc)
    out_s = _trunk(x_sample, p_sample, pos_s, state_a_conv, state_a_rec, state_b_rec, state_d_rec,
                   state_ffn_conv, W, dims, paged=(cache_k4, cache_v4, page_table))

    res = []
    for a, s in zip(out_p, out_s):
        res += [a, s]
    return tuple(res)
```

```python
import math

import jax
import jax.numpy as jnp
from jax import lax
from jax.experimental import pallas as pl
from jax.experimental.pallas import tpu as pltpu

F32 = jnp.float32
BF16 = jnp.bfloat16
EPS = 1e-6
LANES = 128
BF16_SUBLANES = 16
CHUNK = 64
SUB = 16
RET_THETA = 10000.0
TAU_D = 16.0
VMEM_LIMIT = 56 * 1024 * 1024


def _cparams(*sem):
    return pltpu.CompilerParams(dimension_semantics=sem, vmem_limit_bytes=VMEM_LIMIT)


def _pick(n, pref, mult=LANES):
    t = (min(pref, n) // mult) * mult
    while t >= mult:
        if n % t == 0:
            return t
        t -= mult
    return n


def _act_dtype(t):
    return BF16 if t % BF16_SUBLANES == 0 else F32


def _dot(a, b):
    return jnp.dot(a, b, preferred_element_type=F32)


def _dot_nt(a, b):
    return lax.dot_general(a, b, (((1,), (1,)), ((), ())), preferred_element_type=F32)


def _dot_tn(a, b):
    return lax.dot_general(a, b, (((0,), (0,)), ((), ())), preferred_element_type=F32)


def _bf(x):
    return x.astype(BF16)


def _silu(x):
    return x * jax.nn.sigmoid(x)


def _softplus(x):
    return jnp.maximum(x, 0.0) + jnp.log1p(jnp.exp(-jnp.abs(x)))


def _log_sigmoid(x):
    return jnp.minimum(x, 0.0) - jnp.log1p(jnp.exp(-jnp.abs(x)))


def _gelu_tanh(x):
    c = math.sqrt(2.0 / math.pi)
    return 0.5 * x * (1.0 + jnp.tanh(c * (x + 0.044715 * (x * x * x))))


def _iota2(shape, dim):
    return lax.broadcasted_iota(jnp.int32, shape, dim)


def _split2(x):
    hi = x.astype(BF16)
    return hi, (x - hi.astype(F32)).astype(BF16)


def _split3(x):
    hi = x.astype(BF16)
    r1 = x - hi.astype(F32)
    mid = r1.astype(BF16)
    lo = (r1 - mid.astype(F32)).astype(BF16)
    return hi, mid, lo


def _dot01(m01, x):
    hi, mid, lo = _split3(x)
    return _dot(m01, hi) + _dot(m01, mid) + _dot(m01, lo)


def _rms_cast(x2, g):
    m, d = x2.shape
    tm = _pick(m, 256, 8)

    def kern(x_ref, g_ref, o_ref):
        x = x_ref[...]
        r = lax.rsqrt(jnp.mean(x * x, axis=-1, keepdims=True) + EPS)
        o_ref[...] = (x * r * g_ref[...]).astype(o_ref.dtype)

    return pl.pallas_call(
        kern,
        grid=(m // tm,),
        in_specs=[pl.BlockSpec((tm, d), lambda i: (i, 0)), pl.BlockSpec((1, d), lambda i: (0, 0))],
        out_specs=pl.BlockSpec((tm, d), lambda i: (i, 0)),
        out_shape=jax.ShapeDtypeStruct((m, d), BF16),
        compiler_params=_cparams("parallel"),
        name="rms_cast",
    )(x2, g.reshape(1, d))


def _matmul(x, w, *, res=None, ple=None, tm_pref=1024, tn_pref=512, tk=None):
    m, kdim = x.shape
    n = w.shape[1]
    tm = _pick(m, tm_pref, 8)
    tn = _pick(n, tn_pref)
    tk = kdim if tk is None else tk
    nk = kdim // tk
    assert kdim % tk == 0 and m % tm == 0 and n % tn == 0

    in_specs = [pl.BlockSpec((tm, tk), lambda i, j, k: (i, k)), pl.BlockSpec((tk, tn), lambda i, j, k: (k, j))]
    args = [x, w]
    if res is not None:
        in_specs.append(pl.BlockSpec((tm, tn), lambda i, j, k: (i, j)))
        args.append(res)
    if ple is not None:
        h, p, w_in = ple
        pdim = p.shape[1]
        in_specs += [pl.BlockSpec((tm, tn), lambda i, j, k: (i, j)),
                     pl.BlockSpec((tm, pdim), lambda i, j, k: (i, 0)),
                     pl.BlockSpec((pdim, tn), lambda i, j, k: (0, j))]
        args += [h, p, w_in]

    def kern(*refs):
        x_ref, w_ref = refs[0], refs[1]
        o_ref = refs[len(args)]
        acc_ref = refs[len(args) + 1] if nk > 1 else None

        def epilogue(acc):
            if res is not None:
                return refs[2][...] + acc
            if ple is not None:
                h_ref, p_ref, wi_ref = refs[2], refs[3], refs[4]
                emb = _dot(p_ref[...], wi_ref[...])
                return h_ref[...] + jax.nn.sigmoid(acc) * emb
            return acc

        part = _dot(x_ref[...], _bf(w_ref[...]))
        if nk == 1:
            o_ref[...] = epilogue(part)
        else:
            k = pl.program_id(2)

            @pl.when(k == 0)
            def _():
                acc_ref[...] = part

            @pl.when(jnp.logical_and(k > 0, k < nk - 1))
            def _():
                acc_ref[...] += part

            @pl.when(k == nk - 1)
            def _():
                o_ref[...] = epilogue(acc_ref[...] + part)

    return pl.pallas_call(
        kern,
        grid=(m // tm, n // tn, nk),
        in_specs=in_specs,
        out_specs=pl.BlockSpec((tm, tn), lambda i, j, k: (i, j)),
        out_shape=jax.ShapeDtypeStruct((m, n), F32),
        scratch_shapes=[pltpu.VMEM((tm, tn), F32)] if nk > 1 else [],
        compiler_params=_cparams("parallel", "parallel", "arbitrary"),
        name="matmul",
    )(*args)


def _matmul_pair(xa, xb, w, res):
    m, kh = xa.shape
    n = w.shape[1]
    tm = _pick(m, 1024, 8)
    tn = _pick(n, 512)
    assert xb.shape == (m, kh) and w.shape[0] == 2 * kh

    def kern(xa_ref, xb_ref, wa_ref, wb_ref, res_ref, o_ref):
        o_ref[...] = res_ref[...] + (_dot(xa_ref[...], wa_ref[...]) + _dot(xb_ref[...], wb_ref[...]))

    return pl.pallas_call(
        kern,
        grid=(m // tm, n // tn),
        in_specs=[pl.BlockSpec((tm, kh), lambda i, j: (i, 0)), pl.BlockSpec((tm, kh), lambda i, j: (i, 0)),
                  pl.BlockSpec((kh, tn), lambda i, j: (0, j)), pl.BlockSpec((kh, tn), lambda i, j: (1, j)),
                  pl.BlockSpec((tm, tn), lambda i, j: (i, j))],
        out_specs=pl.BlockSpec((tm, tn), lambda i, j: (i, j)),
        out_shape=jax.ShapeDtypeStruct((m, n), F32),
        compiler_params=_cparams("parallel", "parallel"),
        name="matmul_pair",
    )(xa, xb, w, w, res)


def _load_chunk(ref, r0, t):
    if t >= CHUNK:
        return ref[pl.ds(r0, CHUNK), :]
    x = ref[...]
    return jnp.concatenate([x, jnp.zeros((CHUNK - t, x.shape[1]), x.dtype)], axis=0)


def _store_chunk(ref, r0, t, lanes, val):
    if t >= CHUNK:
        ref[pl.ds(r0, CHUNK), lanes] = val.astype(ref.dtype)
    else:
        ref[:, lanes] = val[:t].astype(ref.dtype)


def _time_tile(t):
    tt = min(t, 512)
    assert t % tt == 0 and (tt % CHUNK == 0 or t < CHUNK)
    return tt


def _inv_unit_lower(a, eye, same_sub, nb):
    a_d = [jnp.where(same_sub, x, 0.0) for x in a]
    low = [x - y for x, y in zip(a, a_d)]
    dinv = [eye - x for x in a_d]
    pb = [_bf(-x) for x in a_d]
    s = 2
    while s < SUB:
        pb = [_bf(_dot(x, x)) for x in pb]
        dinv = [d + _dot(_bf(d), x) for d, x in zip(dinv, pb)]
        s *= 2
    if nb == 1:
        return dinv
    db = [_bf(d) for d in dinv]
    nmat = [_dot(d, _bf(x)) for d, x in zip(db, low)]
    nb16 = [_bf(x) for x in nmat]
    acc = [eye - x for x in nmat]
    pw = nb16
    sign = -1.0
    for _ in range(2, nb):
        pwm = [_dot(x, y) for x, y in zip(pw, nb16)]
        pw = [_bf(x) for x in pwm]
        sign = -sign
        acc = [x + sign * y for x, y in zip(acc, pwm)]
    return [_dot(_bf(x), d) for x, d in zip(acc, db)]


def _conv_a_prep(proj3, buf, conv_w, *, qa, dk):
    b, t, _ = proj3.shape
    kc, ch = conv_w.shape
    tc = _pick(qa, 512)
    nqb = qa // tc
    assert ch % tc == 0
    off = 8 - (kc - 1)

    def kern(x_ref, buf_ref, w_ref, o_ref, cn_ref, xs):
        j = pl.program_id(1)
        xs[off:8, :] = buf_ref[...]
        xs[8:8 + t, :] = x_ref[...]
        cn_ref[...] = xs[t + off:t + 8, :]
        is_q = j < nqb
        is_k = j < 2 * nqb
        scale = jnp.where(is_q, dk ** -0.5, 1.0).astype(F32)
        for hh in range(tc // LANES):
            sl = slice(hh * LANES, (hh + 1) * LANES)
            y = xs[off:off + t, sl] * w_ref[0:1, sl]
            for kk in range(1, kc):
                y = y + xs[off + kk:off + kk + t, sl] * w_ref[kk:kk + 1, sl]
            s = _silu(y)
            nrm = s * lax.rsqrt(jnp.sum(s * s, axis=-1, keepdims=True) + EPS) * scale
            o_ref[:, sl] = jnp.where(is_k, nrm, s)

    return pl.pallas_call(
        kern,
        grid=(b, ch // tc),
        in_specs=[pl.BlockSpec((None, t, tc), lambda i, j: (i, 0, j)),
                  pl.BlockSpec((None, kc - 1, tc), lambda i, j: (i, 0, j)),
                  pl.BlockSpec((kc, tc), lambda i, j: (0, j))],
        out_specs=[pl.BlockSpec((None, t, tc), lambda i, j: (i, 0, j)),
                   pl.BlockSpec((None, kc - 1, tc), lambda i, j: (i, 0, j))],
        out_shape=[jax.ShapeDtypeStruct((b, t, ch), F32), jax.ShapeDtypeStruct((b, kc - 1, ch), F32)],
        scratch_shapes=[pltpu.VMEM((t + 8, tc), F32)],
        compiler_params=_cparams("parallel", "parallel"),
        name="conv_a_prep",
    )(proj3, buf, conv_w)


def _gated_delta(qkv3, sp3, proj3, gate_blk0, alog_p, dtb_p, a_norm, s0, *, nh):
    b, t, _ = qkv3.shape
    c = CHUNK
    hs = min(8, nh)
    gh = min(2, hs)
    gs = hs // gh
    n = gh * c
    tt = _time_tile(t)
    nchunks = max(tt // c, 1)
    wl = hs * LANES
    assert nh % hs == 0 and hs % gh == 0 and gate_blk0 % hs == 0
    odt = _act_dtype(t)

    def kern(q_ref, k_ref, v_ref, sp_ref, gate_ref, alog_ref, dtb_ref, an_ref, s0_ref, o_ref, s_ref):
        hb = pl.program_id(1)
        ti = pl.program_id(2)

        @pl.when(ti == 0)
        def _():
            s_ref[...] = s0_ref[...]

        row = _iota2((n, n), 0)
        col = _iota2((n, n), 1)
        same = (row // c) == (col // c)
        incl = jnp.logical_and(same, col <= row)
        strict = jnp.logical_and(same, col < row)
        same_sub = (row // SUB) == (col // SUB)
        lbd = incl.astype(BF16)
        eye = (row == col).astype(F32)
        lane = _iota2((n, LANES), 1)
        rown = _iota2((n, 1), 0)
        valid = (rown % c) < t
        neg_a = -jnp.exp(alog_ref[...])
        dtb = dtb_ref[...]
        an = an_ref[...]

        def stack(x, g0):
            return jnp.concatenate([x[:, (g0 + i) * LANES:(g0 + i + 1) * LANES] for i in range(gh)], axis=0)

        def chunk(ci, carry):
            r0 = pl.multiple_of(ci * c, c)
            qc = _load_chunk(q_ref, r0, tt)
            kc = _load_chunk(k_ref, r0, tt)
            vc = _load_chunk(v_ref, r0, tt)
            gc = _load_chunk(gate_ref, r0, tt)
            sp = _load_chunk(sp_ref, r0, tt)
            beta_all = jax.nn.sigmoid(sp)
            g_all = neg_a * _softplus(sp + dtb)
            bt = jnp.concatenate([beta_all] * gh, axis=0)
            gt = jnp.concatenate([g_all] * gh, axis=0)
            grp = range(gs)
            rowhead = [hb * hs + gi * gh + rown // c for gi in grp]
            q = [stack(qc, gi * gh) for gi in grp]
            k = [stack(kc, gi * gh) for gi in grp]
            v = [stack(vc, gi * gh) for gi in grp]
            s_old = [[s_ref[gi * gh + i] for i in range(gh)] for gi in grp]
            bcol = [jnp.sum(jnp.where(jnp.logical_and(lane == rh, valid), bt, 0.0), axis=1, keepdims=True)
                    for rh in rowhead]
            g_m = [jnp.where(jnp.logical_and(lane == nh + rh, valid), gt, 0.0) for rh in rowhead]
            gcol = [jnp.sum(_dot01(lbd, x), axis=1, keepdims=True) for x in g_m]
            grow = [jnp.sum(eye * x, axis=0, keepdims=True) for x in gcol]
            decay = [jnp.where(incl, jnp.exp(jnp.where(incl, x - y, 0.0)), 0.0) for x, y in zip(gcol, grow)]
            eg = [jnp.exp(x) for x in gcol]
            kb = [_bf(x) for x in k]
            kk = [_dot_nt(x, x) for x in kb]
            amat = [jnp.where(strict, x * d * bc, 0.0) for x, d, bc in zip(kk, decay, bcol)]
            inv = _inv_unit_lower(amat, eye, same_sub, c // SUB)
            rhs = [jnp.concatenate([vv * bc, kx * (bc * e)], axis=1)
                   for vv, kx, bc, e in zip(v, k, bcol, eg)]
            sol = [_dot(_bf(x), _bf(y)) for x, y in zip(inv, rhs)]
            qe = [x * e for x, e in zip(q, eg)]
            sob = [[_bf(x) for x in row_] for row_ in s_old]
            u = [jnp.concatenate([sol[gi][i * c:(i + 1) * c, :LANES]
                                  - _dot(_bf(sol[gi][i * c:(i + 1) * c, LANES:]), sob[gi][i])
                                  for i in range(gh)], axis=0) for gi in grp]
            oi = [jnp.concatenate([_dot(_bf(qe[gi][i * c:(i + 1) * c]), sob[gi][i]) for i in range(gh)], axis=0)
                  for gi in grp]
            ub = [_bf(x) for x in u]
            attn = [_dot_nt(_bf(x), y) * d for x, y, d in zip(q, kb, decay)]
            o = [x + _dot(_bf(y), z) for x, y, z in zip(oi, attn, ub)]
            on = [x * lax.rsqrt(jnp.mean(x * x, axis=-1, keepdims=True) + EPS) * an for x in o]
            for gi in grp:
                for i in range(gh):
                    rs = slice(i * c, (i + 1) * c)
                    lanes = slice((gi * gh + i) * LANES, (gi * gh + i + 1) * LANES)
                    g_last = gcol[gi][(i + 1) * c - 1:(i + 1) * c, :]
                    k_dec = k[gi][rs] * jnp.exp(g_last - gcol[gi][rs])
                    s_ref[gi * gh + i] = s_old[gi][i] * jnp.exp(g_last) + _dot_tn(_bf(k_dec), ub[gi][rs])
                    _store_chunk(o_ref, r0, tt, lanes, on[gi][rs] * _silu(gc[:, lanes]))
            return carry

        if nchunks == 1:
            chunk(0, 0)
        else:
            lax.fori_loop(0, nchunks, chunk, 0)

    blk = lambda off: pl.BlockSpec((None, tt, wl), lambda i, j, ti: (i, ti, off + j))
    vec = pl.BlockSpec((1, LANES), lambda i, j, ti: (0, 0))
    st = pl.BlockSpec((None, hs, LANES, LANES), lambda i, j, ti: (i, j, 0, 0))
    return pl.pallas_call(
        kern,
        grid=(b, nh // hs, t // tt),
        in_specs=[blk(0), blk(nh // hs), blk(2 * nh // hs),
                  pl.BlockSpec((None, tt, LANES), lambda i, j, ti: (i, ti, 0)),
                  blk(gate_blk0 // hs), vec, vec, vec, st],
        out_specs=[blk(0), st],
        out_shape=[jax.ShapeDtypeStruct((b, t, nh * LANES), odt),
                   jax.ShapeDtypeStruct((b, nh, LANES, LANES), F32)],
        compiler_params=_cparams("parallel", "parallel", "arbitrary"),
        name="gated_delta",
    )(qkv3, qkv3, qkv3, sp3, proj3, alog_p, dtb_p, a_norm, s0)


def _gla(proj3, q_blk0, k_blk0, v_blk0, gate_blk0, gain, s0, *, nh, dv, k_scale, norm,
         rot=None, ld3=None, lowrank=None):
    b, t, _ = proj3.shape
    c = CHUNK
    sb = SUB
    nsb = c // sb
    dk = LANES
    hg = min(4, nh)
    tt = _time_tile(t)
    nchunks = max(tt // c, 1)
    assert nh % hg == 0 and q_blk0 % hg == 0 and k_blk0 % hg == 0 and v_blk0 % hg == 0 and gate_blk0 % hg == 0
    odt = _act_dtype(t)
    n_in = 5 + (2 if rot is not None else 0) + (1 if ld3 is not None else 0) + (3 if lowrank is not None else 0)

    def kern(*refs):
        q_ref, k_ref, v_ref, gate_ref, gain_ref = refs[:5]
        pos = 5
        if rot is not None:
            cos_ref, sin_ref = refs[pos], refs[pos + 1]
            pos += 2
        if ld3 is not None:
            ld_ref = refs[pos]
            pos += 1
        if lowrank is not None:
            sp_ref, w2_ref, ab_ref = refs[pos:pos + 3]
            pos += 3
        s0_ref = refs[pos]
        o_ref, s_ref = refs[n_in + 1], refs[n_in + 2]
        st_ref = refs[n_in + 3]
        ti = pl.program_id(2)

        @pl.when(ti == 0)
        def _():
            for hh in range(hg):
                st_ref[hh] = s0_ref[hh].T

        row = _iota2((c, c), 0)
        col = _iota2((c, c), 1)
        lincl = (col <= row).astype(BF16)
        same = (row // sb) == (col // sb)
        diag_mask = jnp.logical_and(same, col <= row)
        below = (col // sb) < (row // sb)
        colmasks = [jnp.logical_and(same, (col % sb) == j) for j in range(sb)]
        rvalid = _iota2((c, 1), 0) < t

        def heads(q, k, v, ld, gate):
            hd = range(hg)
            g = [_dot01(lincl, x) for x in ld]
            st = [st_ref[hh] for hh in hd]
            o = [_dot_nt(_bf(x * jnp.exp(y)), _bf(s)) for x, y, s in zip(q, g, st)]
            offs = [[jnp.zeros((sb, c), F32)] for _ in hd]
            for a in range(1, nsb):
                gs = [x[a * sb - 1:a * sb, :] for x in g]
                qt = [x[a * sb:(a + 1) * sb, :] * jnp.exp(y[a * sb:(a + 1) * sb, :] - z)
                      for x, y, z in zip(q, g, gs)]
                kt = [x * jnp.exp(jnp.minimum(z - y, 0.0)) for x, y, z in zip(k, g, gs)]
                for hh in hd:
                    offs[hh].append(_dot_nt(_bf(qt[hh]), _bf(kt[hh])))
            s_off = [jnp.concatenate(x, axis=0) for x in offs]
            g3 = [x.reshape(nsb, sb, dk) for x in g]
            q3 = [x.reshape(nsb, sb, dk) for x in q]
            k3 = [x.reshape(nsb, sb, dk) for x in k]
            pd = [jnp.zeros((c, c), F32) for _ in hd]
            for j in range(sb):
                e = [jnp.exp(jnp.minimum(x - x[:, j:j + 1, :], 0.0)) for x in g3]
                colj = [jnp.sum(x * y[:, j:j + 1, :] * z, axis=-1, keepdims=True).reshape(c, 1)
                        for x, y, z in zip(q3, k3, e)]
                pd = [jnp.where(colmasks[j], x, y) for x, y in zip(colj, pd)]
            scores = [jnp.where(diag_mask, x, jnp.where(below, y, 0.0)) for x, y in zip(pd, s_off)]
            vb = [_bf(x) for x in v]
            o = [x + _dot(_bf(y), z) for x, y, z in zip(o, scores, vb)]
            g_last = [x[c - 1:c, :] for x in g]
            k_dec = [x * jnp.exp(y - z) for x, y, z in zip(k, g_last, g)]
            for hh in hd:
                st_ref[hh] = st[hh] * jnp.exp(g_last[hh]) + _dot_tn(vb[hh], _bf(k_dec[hh]))
            if norm == "group":
                oc = [x - jnp.mean(x, axis=-1, keepdims=True) for x in o]
                on = [x * lax.rsqrt(jnp.mean(x * x, axis=-1, keepdims=True) + EPS) for x in oc]
            else:
                on = [x * lax.rsqrt(jnp.mean(x * x, axis=-1, keepdims=True) + EPS) for x in o]
            return [on[hh] * gain_ref[hh] * _silu(gate[hh]) for hh in hd]

        def chunk(ci, carry):
            r0 = pl.multiple_of(ci * c, c)
            qc = _load_chunk(q_ref, r0, tt)
            kc = _load_chunk(k_ref, r0, tt)
            vc = _load_chunk(v_ref, r0, tt)
            gatec = _load_chunk(gate_ref, r0, tt)
            if rot is not None:
                cs = _load_chunk(cos_ref, r0, tt)
                sn = _load_chunk(sin_ref, r0, tt)
            if ld3 is not None:
                ldc = _load_chunk(ld_ref, r0, tt)
            else:
                sp = _load_chunk(sp_ref, r0, tt)
                logit = _dot(_bf(sp), _bf(w2_ref[...])) + ab_ref[...]
                ldc = jnp.where(rvalid, _log_sigmoid(logit) / TAU_D, 0.0)
            ks = [slice(hh * dk, (hh + 1) * dk) for hh in range(hg)]
            vs = [slice(hh * dv, (hh + 1) * dv) for hh in range(hg)]
            q = [qc[:, s] for s in ks]
            k = [kc[:, s] for s in ks]
            if rot is not None:
                q = [x * cs + pltpu.roll(x, dk // 2, axis=1) * sn for x in q]
                k = [x * cs + pltpu.roll(x, dk // 2, axis=1) * sn for x in k]
            k = [x * k_scale for x in k]
            out = heads(q, k, [vc[:, s] for s in vs], [ldc[:, s] for s in ks], [gatec[:, s] for s in vs])
            for hh in range(hg):
                _store_chunk(o_ref, r0, tt, vs[hh], out[hh])
            return carry

        if nchunks == 1:
            chunk(0, 0)
        else:
            lax.fori_loop(0, nchunks, chunk, 0)

        @pl.when(ti == pl.num_programs(2) - 1)
        def _():
            for hh in range(hg):
                s_ref[hh] = st_ref[hh].T

    def blk(off, width):
        return pl.BlockSpec((None, tt, hg * width), lambda i, j, ti: (i, ti, off // hg + j))

    in_specs = [blk(q_blk0, dk), blk(k_blk0, dk), blk(v_blk0, dv), blk(gate_blk0, dv),
                pl.BlockSpec((hg, 1, dv), lambda i, j, ti: (j, 0, 0))]
    args = [proj3, proj3, proj3, proj3, gain]
    if rot is not None:
        in_specs += [pl.BlockSpec((tt, dk), lambda i, j, ti: (ti, 0))] * 2
        args += list(rot)
    if ld3 is not None:
        in_specs.append(blk(0, dk))
        args.append(ld3)
    if lowrank is not None:
        sp3, w2p, ab = lowrank
        in_specs += [pl.BlockSpec((None, tt, LANES), lambda i, j, ti: (i, ti, 0)),
                     pl.BlockSpec((LANES, hg * dk), lambda i, j, ti: (0, j)),
                     pl.BlockSpec((1, hg * dk), lambda i, j, ti: (0, j))]
        args += [sp3, w2p, ab]
    st_spec = pl.BlockSpec((None, hg, dk, dv), lambda i, j, ti: (i, j, 0, 0))
    in_specs.append(st_spec)
    args.append(s0)
    assert len(args) == n_in + 1

    return pl.pallas_call(
        kern,
        grid=(b, nh // hg, t // tt),
        in_specs=in_specs,
        out_specs=[blk(0, dv), st_spec],
        out_shape=[jax.ShapeDtypeStruct((b, t, nh * dv), odt),
                   jax.ShapeDtypeStruct((b, nh, dk, dv), F32)],
        scratch_shapes=[pltpu.VMEM((hg, dv, dk), F32)],
        compiler_params=_cparams("parallel", "parallel", "arbitrary"),
        name="gla_" + norm,
    )(*args)


def _head_rms(proj3, blk0, gain, *, nh):
    b, t, _ = proj3.shape
    width = nh * LANES
    tc = _pick(width, 512)
    per = tc // LANES
    c0 = blk0 // per
    assert blk0 % per == 0

    def kern(x_ref, g_ref, o_ref):
        g = g_ref[...]
        for hh in range(per):
            sl = slice(hh * LANES, (hh + 1) * LANES)
            x = x_ref[:, sl]
            o_ref[:, sl] = x * lax.rsqrt(jnp.mean(x * x, axis=-1, keepdims=True) + EPS) * g

    return pl.pallas_call(
        kern,
        grid=(b, width // tc),
        in_specs=[pl.BlockSpec((None, t, tc), lambda i, j: (i, 0, c0 + j)),
                  pl.BlockSpec((1, LANES), lambda i, j: (0, 0))],
        out_specs=pl.BlockSpec((None, t, tc), lambda i, j: (i, 0, j)),
        out_shape=jax.ShapeDtypeStruct((b, t, width), F32),
        compiler_params=_cparams("parallel", "parallel"),
        name="head_rms",
    )(proj3, gain.reshape(1, LANES))


def _sb_logs(z, masks, u):
    sp = [jnp.maximum(x, 0.0) + jnp.log(1.0 + jnp.exp(-jnp.abs(x))) for x in z]
    l1 = [-x if m is None else jnp.where(m, -x, 0.0) for x, m in zip(sp, masks)]
    rsum = [jnp.sum(x, axis=1, keepdims=True) for x in l1]
    parts = [_split2(x) for x in l1]
    suf = [_dot(hi, u) + _dot(lo, u) for hi, lo in parts]
    return sp, rsum, suf


def _sb_weight(z, sp, suf, r_run, mask):
    w = jnp.exp((z - sp) + (suf + r_run))
    return w if mask is None else jnp.where(mask, w, 0.0)


def _sb_prompt(qn3, kn3, proj3, v_blk0, bias, *, nh):
    b, t, _ = qn3.shape
    bq = min(256, t)
    bk = min(128, t)
    ratio = bq // bk
    hp = min(2, nh)
    assert t % bq == 0 and bq % bk == 0 and nh % hp == 0 and v_blk0 % hp == 0
    scale = LANES ** -0.5

    def kern(bias_ref, q_ref, k_ref, v_ref, o_ref):
        hb = pl.program_id(1)
        qi = pl.program_id(2)
        u = (_iota2((bk, bk), 0) > _iota2((bk, bk), 1)).astype(BF16)
        dmask = [_iota2((bq, bk), 1) + s * bk < _iota2((bq, bk), 0) for s in range(ratio)]
        qs = [_bf(q_ref[:, hh * LANES:(hh + 1) * LANES]) for hh in range(hp)]
        biases = [bias_ref[hb * hp + hh] for hh in range(hp)]

        def step(kblk, carry, masks):
            pairs = [(hh, s) for hh in range(hp) for s in range(ratio - 1, -1, -1)]
            k0 = {s: pl.multiple_of(kblk * bq + s * bk, bk) for s in range(ratio)}
            lanes = [slice(hh * LANES, (hh + 1) * LANES) for hh in range(hp)]
            kk = [_bf(k_ref[pl.ds(k0[s], bk), lanes[hh]]) for hh, s in pairs]
            vv = [_bf(v_ref[pl.ds(k0[s], bk), lanes[hh]]) for hh, s in pairs]
            z = [_dot_nt(qs[hh], x) * scale + biases[hh] for (hh, s), x in zip(pairs, kk)]
            pmask = [None if masks is None else masks[s] for hh, s in pairs]
            sp, rsum, suf = _sb_logs(z, pmask, u)
            new = []
            for hh in range(hp):
                r_run, acc = carry[hh]
                for idx, (h2, s) in enumerate(pairs):
                    if h2 != hh:
                        continue
                    a = _sb_weight(z[idx], sp[idx], suf[idx], r_run, pmask[idx])
                    acc = acc + _dot(_bf(a), vv[idx])
                    r_run = r_run + rsum[idx]
                new.append((r_run, acc))
            return tuple(new)

        init = tuple((jnp.zeros((bq, 1), F32), jnp.zeros((bq, LANES), F32)) for _ in range(hp))
        carry = step(qi, init, dmask)
        carry = lax.fori_loop(0, qi, lambda i, cr: step(qi - 1 - i, cr, None), carry)
        for hh in range(hp):
            o_ref[:, hh * LANES:(hh + 1) * LANES] = carry[hh][1].astype(o_ref.dtype)

    wl = hp * LANES
    return pl.pallas_call(
        kern,
        grid=(b, nh // hp, t // bq),
        in_specs=[pl.BlockSpec(memory_space=pltpu.SMEM),
                  pl.BlockSpec((None, bq, wl), lambda i, j, qq: (i, qq, j)),
                  pl.BlockSpec((None, t, wl), lambda i, j, qq: (i, 0, j)),
                  pl.BlockSpec((None, t, wl), lambda i, j, qq: (i, 0, v_blk0 // hp + j))],
        out_specs=pl.BlockSpec((None, bq, wl), lambda i, j, qq: (i, qq, j)),
        out_shape=jax.ShapeDtypeStruct((b, t, nh * LANES), _act_dtype(t)),
        compiler_params=_cparams("parallel", "parallel", "arbitrary"),
        name="sb_prompt",
    )(bias, qn3, kn3, proj3)


def _gather_pages(cache_k, cache_v, page_table):
    nl, n_pool, page, nh, dk = cache_k.shape
    b, npages = page_table.shape
    width = nh * dk
    ck = cache_k.reshape(nl, n_pool, page * nh, dk)
    cv = cache_v.reshape(nl, n_pool, page * nh, dk)

    def kern(pt_ref, k_ref, v_ref, ok_ref, ov_ref):
        for hh in range(nh):
            lanes = slice(hh * dk, (hh + 1) * dk)
            ok_ref[:, lanes] = k_ref[pl.ds(hh, page, stride=nh), :].astype(BF16)
            ov_ref[:, lanes] = v_ref[pl.ds(hh, page, stride=nh), :].astype(BF16)

    in_spec = pl.BlockSpec((None, None, page * nh, dk), lambda l, i, p, pt: (l, pt[i, p], 0, 0))
    out_spec = pl.BlockSpec((None, None, page, width), lambda l, i, p, pt: (l, i, p, 0))
    out_sds = jax.ShapeDtypeStruct((nl, b, npages * page, width), BF16)
    return pl.pallas_call(
        kern,
        grid_spec=pltpu.PrefetchScalarGridSpec(
            num_scalar_prefetch=1, grid=(nl, b, npages),
            in_specs=[in_spec, in_spec], out_specs=[out_spec, out_spec]),
        out_shape=[out_sds, out_sds],
        compiler_params=_cparams("parallel", "parallel", "parallel"),
        name="gather_pages",
    )(page_table, ck, cv)


def _sb_paged(qn3, kn3, proj3, v_blk0, bias, past_k, past_v, layer, *, nh):
    b, t, width = qn3.shape
    plen = past_k.shape[2]
    sk = LANES
    kb = _pick(plen, 512)
    nsub = kb // sk
    nblk = plen // kb
    ht = nh * t
    scale = LANES ** -0.5
    bias_rows = jnp.broadcast_to(jnp.repeat(bias, t)[:, None], (ht, sk)).astype(F32)
    assert v_blk0 % nh == 0 and sk >= t and plen % kb == 0 and kb % sk == 0

    def kern(q_ref, kn_ref, vn_ref, bias_ref, pk_ref, pv_ref, o_ref, acc_ref, r_ref, qb_ref):
        p = pl.program_id(1)
        u = (_iota2((sk, sk), 0) > _iota2((sk, sk), 1)).astype(BF16)

        def process(ks, v, mask):
            qb = qb_ref[...]
            z = [_dot_nt(qb, k) * scale + bias_ref[...] for k in ks]
            sp, rsum, suf = _sb_logs(z, [mask] * len(ks), u)
            r_run = r_ref[...]
            a = [None] * len(ks)
            for s in range(len(ks) - 1, -1, -1):
                a[s] = _bf(_sb_weight(z[s], sp[s], suf[s], r_run, mask))
                r_run = r_run + rsum[s]
            acc_ref[...] += _dot(a[0] if len(ks) == 1 else jnp.concatenate(a, axis=1), v)
            r_ref[...] = r_run

        @pl.when(p == 0)
        def _():
            blockmask = (_iota2((ht, width), 0) // t) == (_iota2((ht, width), 1) // LANES)
            qt = jnp.concatenate([q_ref[...]] * nh, axis=0)
            qb_ref[...] = jnp.where(blockmask, qt, 0.0).astype(BF16)
            acc_ref[...] = jnp.zeros_like(acc_ref)
            r_ref[...] = jnp.zeros_like(r_ref)
            pad = jnp.zeros((sk - t, width), F32)
            kpad = _bf(jnp.concatenate([kn_ref[...], pad], axis=0))
            vpad = _bf(jnp.concatenate([vn_ref[...], pad], axis=0))
            mask = _iota2((ht, sk), 1) < (_iota2((ht, sk), 0) % t)
            process([kpad], vpad, mask)

        process([pk_ref[s * sk:(s + 1) * sk, :] for s in range(nsub)], pv_ref[...], None)

        @pl.when(p == nblk - 1)
        def _():
            blockmask = (_iota2((ht, width), 0) // t) == (_iota2((ht, width), 1) // LANES)
            accm = jnp.where(blockmask, acc_ref[...], 0.0)
            out = accm[0:t, :]
            for hh in range(1, nh):
                out = out + accm[hh * t:(hh + 1) * t, :]
            o_ref[...] = out.astype(o_ref.dtype)

    past_spec = pl.BlockSpec((None, None, kb, width), lambda i, p: (layer, i, nblk - 1 - p, 0))
    return pl.pallas_call(
        kern,
        grid=(b, nblk),
        in_specs=[pl.BlockSpec((None, t, width), lambda i, p: (i, 0, 0)),
                  pl.BlockSpec((None, t, width), lambda i, p: (i, 0, 0)),
                  pl.BlockSpec((None, t, width), lambda i, p: (i, 0, v_blk0 // nh)),
                  pl.BlockSpec((ht, sk), lambda i, p: (0, 0)),
                  past_spec, past_spec],
        out_specs=pl.BlockSpec((None, t, width), lambda i, p: (i, 0, 0)),
        out_shape=jax.ShapeDtypeStruct((b, t, width), _act_dtype(t)),
        scratch_shapes=[pltpu.VMEM((ht, width), F32), pltpu.VMEM((ht, 1), F32), pltpu.VMEM((ht, width), BF16)],
        compiler_params=_cparams("parallel", "arbitrary"),
        name="sb_paged",
    )(qn3, kn3, proj3, bias_rows, past_k, past_v)


def _ffn_up(xn, w, conv_w, buf, *, b, t):
    m, kdim = xn.shape
    f = w.shape[1] // 2
    kc = conv_w.shape[0]
    tm = _pick(t, 1024, 8)
    tn = _pick(f, 512)
    tpb = t // tm
    nfb = f // tn
    off = 8 - (kc - 1)
    sub = _pick(tm, 256, 8)
    assert m == b * t and t % tm == 0

    def kern(x_ref, wa_ref, wb_ref, cw_ref, buf_ref, hid_ref, nb_ref, xs, carry):
        i = pl.program_id(0)
        j = pl.program_id(1)
        first = (i % tpb) == 0

        @pl.when(first)
        def _():
            xs[off:8, :] = buf_ref[...]

        @pl.when(jnp.logical_not(first))
        def _():
            xs[off:8, :] = carry[j, off:8, :]

        wa = _bf(wa_ref[...])
        wb = _bf(wb_ref[...])
        for s in range(tm // sub):
            r0 = s * sub
            x = x_ref[r0:r0 + sub, :]
            a = _dot(x, wa)
            gate = _dot(x, wb)
            xs[8 + r0:8 + r0 + sub, :] = a
            y = xs[off + r0:off + r0 + sub, :] * cw_ref[0:1, :]
            for kk in range(1, kc):
                y = y + xs[off + kk + r0:off + kk + r0 + sub, :] * cw_ref[kk:kk + 1, :]
            hid_ref[r0:r0 + sub, :] = (_gelu_tanh(y) * gate).astype(hid_ref.dtype)
        tail = xs[tm + off:tm + 8, :]
        carry[j, off:8, :] = tail
        nb_ref[...] = tail

    return pl.pallas_call(
        kern,
        grid=(m // tm, nfb),
        in_specs=[pl.BlockSpec((tm, kdim), lambda i, j: (i, 0)),
                  pl.BlockSpec((kdim, tn), lambda i, j: (0, j)),
                  pl.BlockSpec((kdim, tn), lambda i, j: (0, nfb + j)),
                  pl.BlockSpec((kc, tn), lambda i, j: (0, j)),
                  pl.BlockSpec((None, kc - 1, tn), lambda i, j: (i // tpb, 0, j))],
        out_specs=[pl.BlockSpec((tm, tn), lambda i, j: (i, j)),
                   pl.BlockSpec((None, kc - 1, tn), lambda i, j: (i // tpb, 0, j))],
        out_shape=[jax.ShapeDtypeStruct((m, f), BF16), jax.ShapeDtypeStruct((b, kc - 1, f), F32)],
        scratch_shapes=[pltpu.VMEM((tm + 8, tn), F32), pltpu.VMEM((nfb, 8, tn), F32)],
        compiler_params=_cparams("arbitrary", "arbitrary"),
        name="ffn_up",
    )(xn, w, w, conv_w, buf)


def _ffn_act(u3, buf, conv_w):
    b, t, two_f = u3.shape
    f = two_f // 2
    kc = conv_w.shape[0]
    tc = _pick(f, 8192)
    nfb = f // tc
    off = 8 - (kc - 1)

    def kern(a_ref, b_ref, buf_ref, w_ref, hid_ref, nb_ref, xs):
        xs[off:8, :] = buf_ref[...]
        xs[8:8 + t, :] = a_ref[...]
        nb_ref[...] = xs[t + off:t + 8, :]
        y = xs[off:off + t, :] * w_ref[0:1, :]
        for kk in range(1, kc):
            y = y + xs[off + kk:off + kk + t, :] * w_ref[kk:kk + 1, :]
        hid_ref[...] = _gelu_tanh(y) * b_ref[...]

    return pl.pallas_call(
        kern,
        grid=(b, nfb),
        in_specs=[pl.BlockSpec((None, t, tc), lambda i, j: (i, 0, j)),
                  pl.BlockSpec((None, t, tc), lambda i, j: (i, 0, nfb + j)),
                  pl.BlockSpec((None, kc - 1, tc), lambda i, j: (i, 0, j)),
                  pl.BlockSpec((kc, tc), lambda i, j: (0, j))],
        out_specs=[pl.BlockSpec((None, t, tc), lambda i, j: (i, 0, j)),
                   pl.BlockSpec((None, kc - 1, tc), lambda i, j: (i, 0, j))],
        out_shape=[jax.ShapeDtypeStruct((b, t, f), F32), jax.ShapeDtypeStruct((b, kc - 1, f), F32)],
        scratch_shapes=[pltpu.VMEM((t + 8, tc), F32)],
        compiler_params=_cparams("parallel", "parallel"),
        name="ffn_act",
    )(u3, u3, buf, conv_w)


def _rot_tables(pos, dk):
    half = dk // 2
    inv = 1.0 / (RET_THETA ** (jnp.arange(half, dtype=F32) / half))
    ang = pos.astype(F32)[:, None] * inv[None, :]
    cos, sin = jnp.cos(ang), jnp.sin(ang)
    return jnp.concatenate([cos, cos], -1), jnp.concatenate([-sin, sin], -1)


def _trunk(x3, p4, pos, conv_a, rec_a, rec_b, rec_d, ffn_buf, W, dims, paged=None):
    b, t, d = x3.shape
    m = b * t
    (h_a, h_b, dv_b, h_c, h_d, dv_d, r_d) = dims
    qa = h_a * LANES
    conv_ch = 3 * qa
    qb = h_b * LANES
    vb = h_b * dv_b
    qc = h_c * LANES
    qd = h_d * LANES
    vd = h_d * dv_d
    depth = W["norm_mix"].shape[0]
    h2 = x3.reshape(m, d)
    cos_t, sin_t = _rot_tables(pos, LANES)
    log_gamma = jnp.log1p(-jnp.exp2(-5.0 - jnp.arange(h_b, dtype=F32)))
    ld_ret = jnp.broadcast_to(jnp.repeat(log_gamma, LANES)[None, None, :], (b, t, qb))
    conv_l, ra_l, rb_l, k_l, v_l, rd_l, ffn_l = [], [], [], [], [], [], []

    for i in range(depth):
        j = i // 2
        xn = _rms_cast(h2, W["norm_mix"][i])
        if i % 2 == 0:
            proj3 = _matmul(xn, W["w_in_even_main"][j]).reshape(b, t, -1)
            sp3 = _matmul(xn, W["w_in_even_small"][j], tn_pref=LANES).reshape(b, t, LANES)
            qkv3, c_new = _conv_a_prep(proj3, conv_a[j], W["conv_a_w"][j], qa=qa, dk=LANES)
            o_a, sa_new = _gated_delta(qkv3, sp3, proj3, conv_ch // LANES, W["alog_p"][j], W["dtb_p"][j],
                                       W["a_out_norm"][j].reshape(1, LANES), rec_a[j], nh=h_a)
            off_qb = conv_ch + qa
            o_b, sb_new = _gla(proj3, off_qb // LANES, (off_qb + qb) // LANES, (off_qb + 2 * qb) // dv_b,
                               (off_qb + 2 * qb + vb) // dv_b, W["b_gn"][j].reshape(h_b, 1, dv_b), rec_b[j],
                               nh=h_b, dv=dv_b, k_scale=LANES ** -0.5, norm="group",
                               rot=(cos_t, sin_t), ld3=ld_ret)
            h2 = _matmul_pair(o_a.astype(BF16).reshape(m, -1), o_b.astype(BF16).reshape(m, -1),
                              W["w_out_even"][j], h2)
            conv_l.append(c_new)
            ra_l.append(sa_new)
            rb_l.append(sb_new)
        else:
            proj3 = _matmul(xn, W["w_in_odd_main"][j]).reshape(b, t, -1)
            sp3 = _matmul(xn, W["w_in_odd_small"][j], tn_pref=LANES).reshape(b, t, LANES)
            qn3 = _head_rms(proj3, 0, W["c_q_norm"][j], nh=h_c)
            kn3 = _head_rms(proj3, h_c, W["c_k_norm"][j], nh=h_c)
            if paged is None:
                o_c = _sb_prompt(qn3, kn3, proj3, 2 * h_c, W["c_logit_bias"][j], nh=h_c)
            else:
                past_k, past_v = paged
                o_c = _sb_paged(qn3, kn3, proj3, 2 * h_c, W["c_logit_bias"][j], past_k, past_v, j, nh=h_c)
            off_qd = 3 * qc
            o_d, sd_new = _gla(proj3, off_qd // LANES, (off_qd + qd) // LANES, (off_qd + 2 * qd) // dv_d,
                               (off_qd + 2 * qd + vd) // dv_d, W["d_gain"][j], rec_d[j],
                               nh=h_d, dv=dv_d, k_scale=LANES ** -0.5, norm="rms",
                               lowrank=(sp3, W["d_w2p"][j], W["d_alpha_b"][j].reshape(1, qd)))
            h2 = _matmul_pair(o_c.astype(BF16).reshape(m, -1), o_d.astype(BF16).reshape(m, -1),
                              W["w_out_odd"][j], h2)
            k_l.append(kn3.reshape(b, t, h_c, LANES))
            v_l.append(proj3[:, :, 2 * qc:3 * qc].reshape(b, t, h_c, LANES))
            rd_l.append(sd_new)
        xn = _rms_cast(h2, W["norm_ffn"][i])
        if t % BF16_SUBLANES == 0:
            hid, fb = _ffn_up(xn, W["w_ffn_in"][i], W["ffn_conv_w"][i], ffn_buf[i], b=b, t=t)
        else:
            u3 = _matmul(xn, W["w_ffn_in"][i]).reshape(b, t, -1)
            hid, fb = _ffn_act(u3, ffn_buf[i], W["ffn_conv_w"][i])
            hid = hid.astype(BF16).reshape(m, -1)
        f = hid.shape[-1]
        tk_f = f // 2 if (f // 2) % LANES == 0 else f
        h2 = _matmul(hid, W["w_ffn_out"][i], res=h2, tk=tk_f)
        ffn_l.append(fb)
        xn = _rms_cast(h2, W["ple_norm"][i])
        h2 = _matmul(xn, W["w_ple_gate"][i], ple=(h2, p4[i].reshape(m, -1).astype(BF16), W["w_ple_in"][i]))
    return (h2.reshape(b, t, d), jnp.stack(conv_l), jnp.stack(ra_l), jnp.stack(rb_l), jnp.stack(k_l),
            jnp.stack(v_l), jnp.stack(rd_l), jnp.stack(ffn_l))


def kernel(x_prompt, x_sample, state_a_conv, state_a_rec, state_b_rec, cache_k, cache_v, state_d_rec, state_ffn_conv, page_table, p_prompt, p_sample, norm_mix, w_in_even, conv_a_w, a_log, dt_bias, a_out_norm, b_gn, w_out_even, w_in_odd, c_q_norm, c_k_norm, c_logit_bias, d_alpha_w2, d_alpha_b, d_out_norm, w_out_odd, norm_ffn, w_ffn_in, ffn_conv_w, w_ffn_out, ple_norm, w_ple_gate, w_ple_in):
    n_even, h_a = a_log.shape
    n_odd, h_c = c_logit_bias.shape
    h_b, dv_b = state_b_rec.shape[2], state_b_rec.shape[4]
    h_d, dv_d = state_d_rec.shape[2], state_d_rec.shape[4]
    r_d = d_alpha_w2.shape[1]
    depth = norm_mix.shape[0]
    conv_ch = conv_a_w.shape[2]
    qa = h_a * LANES
    qc = h_c * LANES
    qd = h_d * LANES
    assert conv_ch == 3 * qa and state_a_rec.shape[3:] == (LANES, LANES)
    assert state_b_rec.shape[3] == LANES and state_d_rec.shape[3] == LANES and cache_k.shape[4] == LANES
    assert 2 * h_a <= LANES and r_d <= LANES
    dims = (h_a, h_b, dv_b, h_c, h_d, dv_d, r_d)

    w_main_e = jnp.concatenate([w_in_even[:, :, :conv_ch], w_in_even[:, :, conv_ch + 2 * h_a:]], axis=-1).astype(BF16)
    w_small_e = jnp.pad(w_in_even[:, :, conv_ch:conv_ch + 2 * h_a],
                        ((0, 0), (0, 0), (0, LANES - 2 * h_a))).astype(BF16)
    n_main_o = 3 * qc + 2 * qd + 2 * h_d * dv_d
    w_main_o = w_in_odd[:, :, :n_main_o].astype(BF16)
    w_small_o = jnp.pad(w_in_odd[:, :, n_main_o:], ((0, 0), (0, 0), (0, LANES - r_d))).astype(BF16)
    pad_ab = lambda v: jnp.pad(v, ((0, 0), (h_a, LANES - 2 * h_a)))[:, None, :]
    W = {
        "norm_mix": norm_mix, "norm_ffn": norm_ffn, "ple_norm": ple_norm,
        "w_in_even_main": w_main_e, "w_in_even_small": w_small_e,
        "w_in_odd_main": w_main_o, "w_in_odd_small": w_small_o,
        "conv_a_w": conv_a_w, "alog_p": pad_ab(a_log), "dtb_p": pad_ab(dt_bias),
        "a_out_norm": a_out_norm, "b_gn": b_gn,
        "w_out_even": w_out_even.astype(BF16), "w_out_odd": w_out_odd.astype(BF16),
        "c_q_norm": c_q_norm, "c_k_norm": c_k_norm, "c_logit_bias": c_logit_bias,
        "d_w2p": jnp.pad(d_alpha_w2, ((0, 0), (0, LANES - r_d), (0, 0))), "d_alpha_b": d_alpha_b,
        "d_gain": jnp.broadcast_to(d_out_norm[:, None, None, :], (n_odd, h_d, 1, dv_d)),
        "w_ffn_in": w_ffn_in, "ffn_conv_w": ffn_conv_w, "w_ffn_out": w_ffn_out.astype(BF16),
        "w_ple_gate": w_ple_gate.astype(BF16), "w_ple_in": w_ple_in.astype(BF16),
    }

    bp, tp = x_prompt.shape[0], x_prompt.shape[1]
    kconv = conv_a_w.shape[1]
    kffn = ffn_conv_w.shape[1]
    f = w_ffn_out.shape[1]
    zeros = lambda *s: jnp.zeros(s, F32)
    out_p = _trunk(x_prompt, p_prompt, jnp.arange(tp, dtype=jnp.int32),
                   zeros(n_even, bp, kconv - 1, conv_ch), zeros(n_even, bp, h_a, LANES, LANES),
                   zeros(n_even, bp, h_b, LANES, dv_b), zeros(n_odd, bp, h_d, LANES, dv_d),
                   zeros(depth, bp, kffn - 1, f), W, dims)

    ts = x_sample.shape[1]
    past_len = page_table.shape[1] * cache_k.shape[2]
    pos_s = past_len + jnp.arange(ts, dtype=jnp.int32)
    out_s = _trunk(x_sample, p_sample, pos_s, state_a_conv, state_a_rec, state_b_rec, state_d_rec,
                   state_ffn_conv, W, dims, paged=_gather_pages(cache_k, cache_v, page_table))

    res = []
    for a, s in zip(out_p, out_s):
        res += [a, s]
    return tuple(res)
```

```python
import math

import jax
import jax.numpy as jnp
from jax import lax
from jax.experimental import pallas as pl
from jax.experimental.pallas import tpu as pltpu

F32 = jnp.float32
BF16 = jnp.bfloat16
EPS = 1e-6
LANES = 128
BF16_SUBLANES = 16
CHUNK = 64
SUB = 16
RET_THETA = 10000.0
TAU_D = 16.0
VMEM_LIMIT = 56 * 1024 * 1024


def _cparams(*sem):
    return pltpu.CompilerParams(dimension_semantics=sem, vmem_limit_bytes=VMEM_LIMIT)


def _pick(n, pref, mult=LANES):
    t = (min(pref, n) // mult) * mult
    while t >= mult:
        if n % t == 0:
            return t
        t -= mult
    return n


def _act_dtype(t):
    return BF16 if t % BF16_SUBLANES == 0 else F32


def _dot(a, b):
    return jnp.dot(a, b, preferred_element_type=F32)


def _dot_nt(a, b):
    return lax.dot_general(a, b, (((1,), (1,)), ((), ())), preferred_element_type=F32)


def _dot_tn(a, b):
    return lax.dot_general(a, b, (((0,), (0,)), ((), ())), preferred_element_type=F32)


def _bf(x):
    return x.astype(BF16)


def _silu(x):
    return x * jax.nn.sigmoid(x)


def _softplus(x):
    return jnp.maximum(x, 0.0) + jnp.log1p(jnp.exp(-jnp.abs(x)))


def _log_sigmoid(x):
    return jnp.minimum(x, 0.0) - jnp.log1p(jnp.exp(-jnp.abs(x)))


def _gelu_tanh(x):
    c = math.sqrt(2.0 / math.pi)
    return 0.5 * x * (1.0 + jnp.tanh(c * (x + 0.044715 * (x * x * x))))


def _iota2(shape, dim):
    return lax.broadcasted_iota(jnp.int32, shape, dim)


def _split3(x):
    hi = x.astype(BF16)
    r1 = x - hi.astype(F32)
    mid = r1.astype(BF16)
    lo = (r1 - mid.astype(F32)).astype(BF16)
    return hi, mid, lo


def _dot01(m01, x):
    hi, mid, lo = _split3(x)
    return _dot(m01, hi) + _dot(m01, mid) + _dot(m01, lo)


def _rms_cast(x2, g):
    m, d = x2.shape
    tm = _pick(m, 512, 8)

    def kern(x_ref, g_ref, o_ref):
        x = x_ref[...]
        r = lax.rsqrt(jnp.mean(x * x, axis=-1, keepdims=True) + EPS)
        o_ref[...] = (x * r * g_ref[...]).astype(o_ref.dtype)

    return pl.pallas_call(
        kern,
        grid=(m // tm,),
        in_specs=[pl.BlockSpec((tm, d), lambda i: (i, 0)), pl.BlockSpec((1, d), lambda i: (0, 0))],
        out_specs=pl.BlockSpec((tm, d), lambda i: (i, 0)),
        out_shape=jax.ShapeDtypeStruct((m, d), BF16),
        compiler_params=_cparams("parallel"),
        name="rms_cast",
    )(x2, g.reshape(1, d))


def _matmul(x, w, layer, *, res=None, ple=None, tm_pref=1024, tn_pref=512, tk=None):
    m, kdim = x.shape
    n = w.shape[2]
    tm = _pick(m, tm_pref, 8)
    tn = _pick(n, tn_pref)
    tk = kdim if tk is None else tk
    nk = kdim // tk
    assert kdim % tk == 0 and m % tm == 0 and n % tn == 0 and w.shape[1] == kdim

    in_specs = [pl.BlockSpec((tm, tk), lambda i, j, k: (i, k)),
                pl.BlockSpec((None, tk, tn), lambda i, j, k: (layer, k, j))]
    args = [x, w]
    if res is not None:
        in_specs.append(pl.BlockSpec((tm, tn), lambda i, j, k: (i, j)))
        args.append(res)
    if ple is not None:
        h, p, w_in = ple
        pdim = p.shape[1]
        in_specs += [pl.BlockSpec((tm, tn), lambda i, j, k: (i, j)),
                     pl.BlockSpec((tm, pdim), lambda i, j, k: (i, 0)),
                     pl.BlockSpec((None, pdim, tn), lambda i, j, k: (layer, 0, j))]
        args += [h, p, w_in]

    def kern(*refs):
        x_ref, w_ref = refs[0], refs[1]
        o_ref = refs[len(args)]
        acc_ref = refs[len(args) + 1] if nk > 1 else None

        def epilogue(acc):
            if res is not None:
                return refs[2][...] + acc
            if ple is not None:
                h_ref, p_ref, wi_ref = refs[2], refs[3], refs[4]
                emb = _dot(p_ref[...], wi_ref[...])
                return h_ref[...] + jax.nn.sigmoid(acc) * emb
            return acc

        part = _dot(x_ref[...], _bf(w_ref[...]))
        if nk == 1:
            o_ref[...] = epilogue(part)
        else:
            k = pl.program_id(2)

            @pl.when(k == 0)
            def _():
                acc_ref[...] = part

            @pl.when(jnp.logical_and(k > 0, k < nk - 1))
            def _():
                acc_ref[...] += part

            @pl.when(k == nk - 1)
            def _():
                o_ref[...] = epilogue(acc_ref[...] + part)

    return pl.pallas_call(
        kern,
        grid=(m // tm, n // tn, nk),
        in_specs=in_specs,
        out_specs=pl.BlockSpec((tm, tn), lambda i, j, k: (i, j)),
        out_shape=jax.ShapeDtypeStruct((m, n), F32),
        scratch_shapes=[pltpu.VMEM((tm, tn), F32)] if nk > 1 else [],
        compiler_params=_cparams("parallel", "parallel", "arbitrary"),
        name="matmul",
    )(*args)


def _matmul_pair(xa, xb, w, layer, res):
    m, kh = xa.shape
    n = w.shape[2]
    tm = _pick(m, 1024, 8)
    tn = _pick(n, 512)
    assert xb.shape == (m, kh) and w.shape[1] == 2 * kh

    def kern(xa_ref, xb_ref, wa_ref, wb_ref, res_ref, o_ref):
        o_ref[...] = res_ref[...] + (_dot(xa_ref[...], _bf(wa_ref[...])) + _dot(xb_ref[...], _bf(wb_ref[...])))

    return pl.pallas_call(
        kern,
        grid=(m // tm, n // tn),
        in_specs=[pl.BlockSpec((tm, kh), lambda i, j: (i, 0)), pl.BlockSpec((tm, kh), lambda i, j: (i, 0)),
                  pl.BlockSpec((None, kh, tn), lambda i, j: (layer, 0, j)),
                  pl.BlockSpec((None, kh, tn), lambda i, j: (layer, 1, j)),
                  pl.BlockSpec((tm, tn), lambda i, j: (i, j))],
        out_specs=pl.BlockSpec((tm, tn), lambda i, j: (i, j)),
        out_shape=jax.ShapeDtypeStruct((m, n), F32),
        compiler_params=_cparams("parallel", "parallel"),
        name="matmul_pair",
    )(xa, xb, w, w, res)


def _load_chunk(ref, r0, t):
    if t >= CHUNK:
        return ref[pl.ds(r0, CHUNK), :]
    x = ref[...]
    return jnp.concatenate([x, jnp.zeros((CHUNK - t, x.shape[1]), x.dtype)], axis=0)


def _store_chunk(ref, r0, t, lanes, val):
    if t >= CHUNK:
        ref[pl.ds(r0, CHUNK), lanes] = val.astype(ref.dtype)
    else:
        ref[:, lanes] = val[:t].astype(ref.dtype)


def _time_tile(t):
    tt = min(t, 512)
    assert t % tt == 0 and (tt % CHUNK == 0 or t < CHUNK)
    return tt


def _inv_unit_lower(a, eye, same_sub, nb):
    a_d = [jnp.where(same_sub, x, 0.0) for x in a]
    low = [x - y for x, y in zip(a, a_d)]
    dinv = [eye - x for x in a_d]
    pb = [_bf(-x) for x in a_d]
    s = 2
    while s < SUB:
        pb = [_bf(_dot(x, x)) for x in pb]
        dinv = [d + _dot(_bf(d), x) for d, x in zip(dinv, pb)]
        s *= 2
    if nb == 1:
        return dinv
    db = [_bf(d) for d in dinv]
    nmat = [_dot(d, _bf(x)) for d, x in zip(db, low)]
    nb16 = [_bf(x) for x in nmat]
    acc = [eye - x for x in nmat]
    pw = nb16
    sign = -1.0
    for _ in range(2, nb):
        pwm = [_dot(x, y) for x, y in zip(pw, nb16)]
        pw = [_bf(x) for x in pwm]
        sign = -sign
        acc = [x + sign * y for x, y in zip(acc, pwm)]
    return [_dot(_bf(x), d) for x, d in zip(acc, db)]


def _conv_a_prep(proj3, buf, conv_w, *, qa, dk):
    b, t, _ = proj3.shape
    kc, ch = conv_w.shape
    tc = _pick(qa, 512)
    nqb = qa // tc
    assert ch % tc == 0
    off = 8 - (kc - 1)

    def kern(x_ref, buf_ref, w_ref, o_ref, cn_ref, xs):
        j = pl.program_id(1)
        xs[off:8, :] = buf_ref[...]
        xs[8:8 + t, :] = x_ref[...]
        cn_ref[...] = xs[t + off:t + 8, :]
        is_q = j < nqb
        is_k = j < 2 * nqb
        scale = jnp.where(is_q, dk ** -0.5, 1.0).astype(F32)
        for hh in range(tc // LANES):
            sl = slice(hh * LANES, (hh + 1) * LANES)
            y = xs[off:off + t, sl] * w_ref[0:1, sl]
            for kk in range(1, kc):
                y = y + xs[off + kk:off + kk + t, sl] * w_ref[kk:kk + 1, sl]
            s = _silu(y)
            nrm = s * lax.rsqrt(jnp.sum(s * s, axis=-1, keepdims=True) + EPS) * scale
            o_ref[:, sl] = jnp.where(is_k, nrm, s)

    return pl.pallas_call(
        kern,
        grid=(b, ch // tc),
        in_specs=[pl.BlockSpec((None, t, tc), lambda i, j: (i, 0, j)),
                  pl.BlockSpec((None, kc - 1, tc), lambda i, j: (i, 0, j)),
                  pl.BlockSpec((kc, tc), lambda i, j: (0, j))],
        out_specs=[pl.BlockSpec((None, t, tc), lambda i, j: (i, 0, j)),
                   pl.BlockSpec((None, kc - 1, tc), lambda i, j: (i, 0, j))],
        out_shape=[jax.ShapeDtypeStruct((b, t, ch), F32), jax.ShapeDtypeStruct((b, kc - 1, ch), F32)],
        scratch_shapes=[pltpu.VMEM((t + 8, tc), F32)],
        compiler_params=_cparams("parallel", "parallel"),
        name="conv_a_prep",
    )(proj3, buf, conv_w)


def _gated_delta(qkv3, sp3, proj3, gate_blk0, alog_p, dtb_p, a_norm, s0, *, nh):
    b, t, _ = qkv3.shape
    c = CHUNK
    hs = min(8, nh)
    gh = min(2, hs)
    gs = hs // gh
    n = gh * c
    tt = _time_tile(t)
    nchunks = max(tt // c, 1)
    wl = hs * LANES
    assert nh % hs == 0 and hs % gh == 0 and gate_blk0 % hs == 0
    odt = _act_dtype(t)

    def kern(q_ref, k_ref, v_ref, sp_ref, gate_ref, alog_ref, dtb_ref, an_ref, s0_ref, o_ref, s_ref):
        hb = pl.program_id(1)
        ti = pl.program_id(2)

        @pl.when(ti == 0)
        def _():
            s_ref[...] = s0_ref[...]

        row = _iota2((n, n), 0)
        col = _iota2((n, n), 1)
        same = (row // c) == (col // c)
        incl = jnp.logical_and(same, col <= row)
        strict = jnp.logical_and(same, col < row)
        same_sub = (row // SUB) == (col // SUB)
        lbd = incl.astype(BF16)
        eye = (row == col).astype(F32)
        lane = _iota2((n, LANES), 1)
        rown = _iota2((n, 1), 0)
        valid = (rown % c) < t
        neg_a = -jnp.exp(alog_ref[...])
        dtb = dtb_ref[...]
        an = an_ref[...]

        def stack(x, g0):
            return jnp.concatenate([x[:, (g0 + i) * LANES:(g0 + i + 1) * LANES] for i in range(gh)], axis=0)

        def chunk(ci, carry):
            r0 = pl.multiple_of(ci * c, c)
            qc = _load_chunk(q_ref, r0, tt)
            kc = _load_chunk(k_ref, r0, tt)
            vc = _load_chunk(v_ref, r0, tt)
            gc = _load_chunk(gate_ref, r0, tt)
            sp = _load_chunk(sp_ref, r0, tt)
            beta_all = jax.nn.sigmoid(sp)
            g_all = neg_a * _softplus(sp + dtb)
            bt = jnp.concatenate([beta_all] * gh, axis=0)
            gt = jnp.concatenate([g_all] * gh, axis=0)
            grp = range(gs)
            rowhead = [hb * hs + gi * gh + rown // c for gi in grp]
            q = [stack(qc, gi * gh) for gi in grp]
            k = [stack(kc, gi * gh) for gi in grp]
            v = [stack(vc, gi * gh) for gi in grp]
            s_old = [[s_ref[gi * gh + i] for i in range(gh)] for gi in grp]
            bcol = [jnp.sum(jnp.where(jnp.logical_and(lane == rh, valid), bt, 0.0), axis=1, keepdims=True)
                    for rh in rowhead]
            g_m = [jnp.where(jnp.logical_and(lane == nh + rh, valid), gt, 0.0) for rh in rowhead]
            gcol = [jnp.sum(_dot01(lbd, x), axis=1, keepdims=True) for x in g_m]
            grow = [jnp.sum(eye * x, axis=0, keepdims=True) for x in gcol]
            decay = [jnp.where(incl, jnp.exp(jnp.where(incl, x - y, 0.0)), 0.0) for x, y in zip(gcol, grow)]
            eg = [jnp.exp(x) for x in gcol]
            kb = [_bf(x) for x in k]
            kk = [_dot_nt(x, x) for x in kb]
            amat = [jnp.where(strict, x * d * bc, 0.0) for x, d, bc in zip(kk, decay, bcol)]
            inv = _inv_unit_lower(amat, eye, same_sub, c // SUB)
            rhs = [jnp.concatenate([vv * bc, kx * (bc * e)], axis=1)
                   for vv, kx, bc, e in zip(v, k, bcol, eg)]
            sol = [_dot(_bf(x), _bf(y)) for x, y in zip(inv, rhs)]
            qe = [x * e for x, e in zip(q, eg)]
            sob = [[_bf(x) for x in row_] for row_ in s_old]
            u = [jnp.concatenate([sol[gi][i * c:(i + 1) * c, :LANES]
                                  - _dot(_bf(sol[gi][i * c:(i + 1) * c, LANES:]), sob[gi][i])
                                  for i in range(gh)], axis=0) for gi in grp]
            oi = [jnp.concatenate([_dot(_bf(qe[gi][i * c:(i + 1) * c]), sob[gi][i]) for i in range(gh)], axis=0)
                  for gi in grp]
            ub = [_bf(x) for x in u]
            attn = [_dot_nt(_bf(x), y) * d for x, y, d in zip(q, kb, decay)]
            o = [x + _dot(_bf(y), z) for x, y, z in zip(oi, attn, ub)]
            on = [x * lax.rsqrt(jnp.mean(x * x, axis=-1, keepdims=True) + EPS) * an for x in o]
            for gi in grp:
                for i in range(gh):
                    rs = slice(i * c, (i + 1) * c)
                    lanes = slice((gi * gh + i) * LANES, (gi * gh + i + 1) * LANES)
                    g_last = gcol[gi][(i + 1) * c - 1:(i + 1) * c, :]
                    k_dec = k[gi][rs] * jnp.exp(g_last - gcol[gi][rs])
                    s_ref[gi * gh + i] = s_old[gi][i] * jnp.exp(g_last) + _dot_tn(_bf(k_dec), ub[gi][rs])
                    _store_chunk(o_ref, r0, tt, lanes, on[gi][rs] * _silu(gc[:, lanes]))
            return carry

        if nchunks == 1:
            chunk(0, 0)
        else:
            lax.fori_loop(0, nchunks, chunk, 0)

    blk = lambda off: pl.BlockSpec((None, tt, wl), lambda i, j, ti: (i, ti, off + j))
    vec = pl.BlockSpec((1, LANES), lambda i, j, ti: (0, 0))
    st = pl.BlockSpec((None, hs, LANES, LANES), lambda i, j, ti: (i, j, 0, 0))
    return pl.pallas_call(
        kern,
        grid=(b, nh // hs, t // tt),
        in_specs=[blk(0), blk(nh // hs), blk(2 * nh // hs),
                  pl.BlockSpec((None, tt, LANES), lambda i, j, ti: (i, ti, 0)),
                  blk(gate_blk0 // hs), vec, vec, vec, st],
        out_specs=[blk(0), st],
        out_shape=[jax.ShapeDtypeStruct((b, t, nh * LANES), odt),
                   jax.ShapeDtypeStruct((b, nh, LANES, LANES), F32)],
        compiler_params=_cparams("parallel", "parallel", "arbitrary"),
        name="gated_delta",
    )(qkv3, qkv3, qkv3, sp3, proj3, alog_p, dtb_p, a_norm, s0)


def _gla(proj3, q_blk0, k_blk0, v_blk0, gate_blk0, gain, s0, *, nh, dv, k_scale, norm,
         rot=None, ld3=None, lowrank=None):
    b, t, _ = proj3.shape
    c = CHUNK
    sb = SUB
    nsb = c // sb
    dk = LANES
    hg = min(4, nh)
    tt = _time_tile(t)
    nchunks = max(tt // c, 1)
    assert nh % hg == 0 and q_blk0 % hg == 0 and k_blk0 % hg == 0 and v_blk0 % hg == 0 and gate_blk0 % hg == 0
    odt = _act_dtype(t)
    n_in = 5 + (2 if rot is not None else 0) + (1 if ld3 is not None else 0) + (3 if lowrank is not None else 0)

    def kern(*refs):
        q_ref, k_ref, v_ref, gate_ref, gain_ref = refs[:5]
        pos = 5
        if rot is not None:
            cos_ref, sin_ref = refs[pos], refs[pos + 1]
            pos += 2
        if ld3 is not None:
            ld_ref = refs[pos]
            pos += 1
        if lowrank is not None:
            sp_ref, w2_ref, ab_ref = refs[pos:pos + 3]
            pos += 3
        s0_ref = refs[pos]
        o_ref, s_ref = refs[n_in + 1], refs[n_in + 2]
        st_ref = refs[n_in + 3]
        ti = pl.program_id(2)

        @pl.when(ti == 0)
        def _():
            for hh in range(hg):
                st_ref[hh] = s0_ref[hh].T

        row = _iota2((c, c), 0)
        col = _iota2((c, c), 1)
        lincl = (col <= row).astype(BF16)
        same = (row // sb) == (col // sb)
        diag_mask = jnp.logical_and(same, col <= row)
        below = (col // sb) < (row // sb)
        colmasks = [jnp.logical_and(same, (col % sb) == j) for j in range(sb)]
        rvalid = _iota2((c, 1), 0) < t

        def heads(q, k, v, ld, gate):
            hd = range(hg)
            g = [_dot01(lincl, x) for x in ld]
            st = [st_ref[hh] for hh in hd]
            o = [_dot_nt(_bf(x * jnp.exp(y)), _bf(s)) for x, y, s in zip(q, g, st)]
            offs = [[jnp.zeros((sb, c), F32)] for _ in hd]
            for a in range(1, nsb):
                gs = [x[a * sb - 1:a * sb, :] for x in g]
                qt = [x[a * sb:(a + 1) * sb, :] * jnp.exp(y[a * sb:(a + 1) * sb, :] - z)
                      for x, y, z in zip(q, g, gs)]
                kt = [x * jnp.exp(jnp.minimum(z - y, 0.0)) for x, y, z in zip(k, g, gs)]
                for hh in hd:
                    offs[hh].append(_dot_nt(_bf(qt[hh]), _bf(kt[hh])))
            s_off = [jnp.concatenate(x, axis=0) for x in offs]
            g3 = [x.reshape(nsb, sb, dk) for x in g]
            q3 = [x.reshape(nsb, sb, dk) for x in q]
            k3 = [x.reshape(nsb, sb, dk) for x in k]
            pd = [jnp.zeros((c, c), F32) for _ in hd]
            for j in range(sb):
                e = [jnp.exp(jnp.minimum(x - x[:, j:j + 1, :], 0.0)) for x in g3]
                colj = [jnp.sum(x * y[:, j:j + 1, :] * z, axis=-1, keepdims=True).reshape(c, 1)
                        for x, y, z in zip(q3, k3, e)]
                pd = [jnp.where(colmasks[j], x, y) for x, y in zip(colj, pd)]
            scores = [jnp.where(diag_mask, x, jnp.where(below, y, 0.0)) for x, y in zip(pd, s_off)]
            vb = [_bf(x) for x in v]
            o = [x + _dot(_bf(y), z) for x, y, z in zip(o, scores, vb)]
            g_last = [x[c - 1:c, :] for x in g]
            k_dec = [x * jnp.exp(y - z) for x, y, z in zip(k, g_last, g)]
            for hh in hd:
                st_ref[hh] = st[hh] * jnp.exp(g_last[hh]) + _dot_tn(vb[hh], _bf(k_dec[hh]))
            if norm == "group":
                oc = [x - jnp.mean(x, axis=-1, keepdims=True) for x in o]
                on = [x * lax.rsqrt(jnp.mean(x * x, axis=-1, keepdims=True) + EPS) for x in oc]
            else:
                on = [x * lax.rsqrt(jnp.mean(x * x, axis=-1, keepdims=True) + EPS) for x in o]
            return [on[hh] * gain_ref[hh] * _silu(gate[hh]) for hh in hd]

        def chunk(ci, carry):
            r0 = pl.multiple_of(ci * c, c)
            qc = _load_chunk(q_ref, r0, tt)
            kc = _load_chunk(k_ref, r0, tt)
            vc = _load_chunk(v_ref, r0, tt)
            gatec = _load_chunk(gate_ref, r0, tt)
            if rot is not None:
                cs = _load_chunk(cos_ref, r0, tt)
                sn = _load_chunk(sin_ref, r0, tt)
            if ld3 is not None:
                ldc = _load_chunk(ld_ref, r0, tt)
            else:
                sp = _load_chunk(sp_ref, r0, tt)
                logit = _dot(_bf(sp), _bf(w2_ref[...])) + ab_ref[...]
                ldc = jnp.where(rvalid, _log_sigmoid(logit) / TAU_D, 0.0)
            ks = [slice(hh * dk, (hh + 1) * dk) for hh in range(hg)]
            vs = [slice(hh * dv, (hh + 1) * dv) for hh in range(hg)]
            q = [qc[:, s] for s in ks]
            k = [kc[:, s] for s in ks]
            if rot is not None:
                q = [x * cs + pltpu.roll(x, dk // 2, axis=1) * sn for x in q]
                k = [x * cs + pltpu.roll(x, dk // 2, axis=1) * sn for x in k]
            k = [x * k_scale for x in k]
            out = heads(q, k, [vc[:, s] for s in vs], [ldc[:, s] for s in ks], [gatec[:, s] for s in vs])
            for hh in range(hg):
                _store_chunk(o_ref, r0, tt, vs[hh], out[hh])
            return carry

        if nchunks == 1:
            chunk(0, 0)
        else:
            lax.fori_loop(0, nchunks, chunk, 0)

        @pl.when(ti == pl.num_programs(2) - 1)
        def _():
            for hh in range(hg):
                s_ref[hh] = st_ref[hh].T

    def blk(off, width):
        return pl.BlockSpec((None, tt, hg * width), lambda i, j, ti: (i, ti, off // hg + j))

    in_specs = [blk(q_blk0, dk), blk(k_blk0, dk), blk(v_blk0, dv), blk(gate_blk0, dv),
                pl.BlockSpec((hg, 1, dv), lambda i, j, ti: (j, 0, 0))]
    args = [proj3, proj3, proj3, proj3, gain]
    if rot is not None:
        in_specs += [pl.BlockSpec((tt, dk), lambda i, j, ti: (ti, 0))] * 2
        args += list(rot)
    if ld3 is not None:
        in_specs.append(blk(0, dk))
        args.append(ld3)
    if lowrank is not None:
        sp3, w2p, ab = lowrank
        in_specs += [pl.BlockSpec((None, tt, LANES), lambda i, j, ti: (i, ti, 0)),
                     pl.BlockSpec((LANES, hg * dk), lambda i, j, ti: (0, j)),
                     pl.BlockSpec((1, hg * dk), lambda i, j, ti: (0, j))]
        args += [sp3, w2p, ab]
    st_spec = pl.BlockSpec((None, hg, dk, dv), lambda i, j, ti: (i, j, 0, 0))
    in_specs.append(st_spec)
    args.append(s0)
    assert len(args) == n_in + 1

    return pl.pallas_call(
        kern,
        grid=(b, nh // hg, t // tt),
        in_specs=in_specs,
        out_specs=[blk(0, dv), st_spec],
        out_shape=[jax.ShapeDtypeStruct((b, t, nh * dv), odt),
                   jax.ShapeDtypeStruct((b, nh, dk, dv), F32)],
        scratch_shapes=[pltpu.VMEM((hg, dv, dk), F32)],
        compiler_params=_cparams("parallel", "parallel", "arbitrary"),
        name="gla_" + norm,
    )(*args)


def _head_rms(proj3, blk0, gain, *, nh):
    b, t, _ = proj3.shape
    width = nh * LANES
    tc = _pick(width, 512)
    per = tc // LANES
    c0 = blk0 // per
    assert blk0 % per == 0

    def kern(x_ref, g_ref, o_ref):
        g = g_ref[...]
        for hh in range(per):
            sl = slice(hh * LANES, (hh + 1) * LANES)
            x = x_ref[:, sl]
            o_ref[:, sl] = x * lax.rsqrt(jnp.mean(x * x, axis=-1, keepdims=True) + EPS) * g

    return pl.pallas_call(
        kern,
        grid=(b, width // tc),
        in_specs=[pl.BlockSpec((None, t, tc), lambda i, j: (i, 0, c0 + j)),
                  pl.BlockSpec((1, LANES), lambda i, j: (0, 0))],
        out_specs=pl.BlockSpec((None, t, tc), lambda i, j: (i, 0, j)),
        out_shape=jax.ShapeDtypeStruct((b, t, width), F32),
        compiler_params=_cparams("parallel", "parallel"),
        name="head_rms",
    )(proj3, gain.reshape(1, LANES))


def _sb_logs(z, masks, u):
    sp = [jnp.maximum(x, 0.0) + jnp.log(1.0 + jnp.exp(-jnp.abs(x))) for x in z]
    l1 = [-x if m is None else jnp.where(m, -x, 0.0) for x, m in zip(sp, masks)]
    rsum = [jnp.sum(x, axis=1, keepdims=True) for x in l1]
    suf = [_dot(_bf(x), u) for x in l1]
    return sp, rsum, suf


def _sb_weight(z, sp, suf, r_run, mask):
    w = jnp.exp((z - sp) + (suf + r_run))
    return w if mask is None else jnp.where(mask, w, 0.0)


def _sb_prompt(qn3, kn3, proj3, v_blk0, bias, *, nh):
    b, t, _ = qn3.shape
    bq = min(256, t)
    bk = min(128, t)
    ratio = bq // bk
    hp = min(2, nh)
    assert t % bq == 0 and bq % bk == 0 and nh % hp == 0 and v_blk0 % hp == 0
    scale = LANES ** -0.5

    def kern(bias_ref, q_ref, k_ref, v_ref, o_ref):
        hb = pl.program_id(1)
        qi = pl.program_id(2)
        u = (_iota2((bk, bk), 0) > _iota2((bk, bk), 1)).astype(BF16)
        dmask = [_iota2((bq, bk), 1) + s * bk < _iota2((bq, bk), 0) for s in range(ratio)]
        qs = [_bf(q_ref[:, hh * LANES:(hh + 1) * LANES]) for hh in range(hp)]
        biases = [bias_ref[hb * hp + hh] for hh in range(hp)]

        def step(kblk, carry, masks):
            pairs = [(hh, s) for hh in range(hp) for s in range(ratio - 1, -1, -1)]
            k0 = {s: pl.multiple_of(kblk * bq + s * bk, bk) for s in range(ratio)}
            lanes = [slice(hh * LANES, (hh + 1) * LANES) for hh in range(hp)]
            kk = [_bf(k_ref[pl.ds(k0[s], bk), lanes[hh]]) for hh, s in pairs]
            vv = [_bf(v_ref[pl.ds(k0[s], bk), lanes[hh]]) for hh, s in pairs]
            z = [_dot_nt(qs[hh], x) * scale + biases[hh] for (hh, s), x in zip(pairs, kk)]
            pmask = [None if masks is None else masks[s] for hh, s in pairs]
            sp, rsum, suf = _sb_logs(z, pmask, u)
            new = []
            for hh in range(hp):
                r_run, acc = carry[hh]
                for idx, (h2, s) in enumerate(pairs):
                    if h2 != hh:
                        continue
                    a = _sb_weight(z[idx], sp[idx], suf[idx], r_run, pmask[idx])
                    acc = acc + _dot(_bf(a), vv[idx])
                    r_run = r_run + rsum[idx]
                new.append((r_run, acc))
            return tuple(new)

        init = tuple((jnp.zeros((bq, 1), F32), jnp.zeros((bq, LANES), F32)) for _ in range(hp))
        carry = step(qi, init, dmask)
        carry = lax.fori_loop(0, qi, lambda i, cr: step(qi - 1 - i, cr, None), carry)
        for hh in range(hp):
            o_ref[:, hh * LANES:(hh + 1) * LANES] = carry[hh][1].astype(o_ref.dtype)

    wl = hp * LANES
    return pl.pallas_call(
        kern,
        grid=(b, nh // hp, t // bq),
        in_specs=[pl.BlockSpec(memory_space=pltpu.SMEM),
                  pl.BlockSpec((None, bq, wl), lambda i, j, qq: (i, qq, j)),
                  pl.BlockSpec((None, t, wl), lambda i, j, qq: (i, 0, j)),
                  pl.BlockSpec((None, t, wl), lambda i, j, qq: (i, 0, v_blk0 // hp + j))],
        out_specs=pl.BlockSpec((None, bq, wl), lambda i, j, qq: (i, qq, j)),
        out_shape=jax.ShapeDtypeStruct((b, t, nh * LANES), _act_dtype(t)),
        compiler_params=_cparams("parallel", "parallel", "arbitrary"),
        name="sb_prompt",
    )(bias, qn3, kn3, proj3)


def _gather_pages(cache_k, cache_v, page_table):
    nl, n_pool, page, nh, dk = cache_k.shape
    b, npages = page_table.shape
    width = nh * dk
    ck = cache_k.reshape(nl, n_pool, page * nh, dk)
    cv = cache_v.reshape(nl, n_pool, page * nh, dk)

    pps = max(d for d in (4, 2, 1) if npages % d == 0)

    def kern(pt_ref, *refs):
        ins, (ok_ref, ov_ref) = refs[:2 * pps], refs[2 * pps:]
        for q in range(pps):
            rows = slice(q * page, (q + 1) * page)
            for hh in range(nh):
                lanes = slice(hh * dk, (hh + 1) * dk)
                ok_ref[rows, lanes] = ins[q][pl.ds(hh, page, stride=nh), :].astype(BF16)
                ov_ref[rows, lanes] = ins[pps + q][pl.ds(hh, page, stride=nh), :].astype(BF16)

    def in_spec(q):
        return pl.BlockSpec((None, None, page * nh, dk), lambda l, i, p, pt: (l, pt[i, pps * p + q], 0, 0))

    out_spec = pl.BlockSpec((None, None, pps * page, width), lambda l, i, p, pt: (l, i, p, 0))
    out_sds = jax.ShapeDtypeStruct((nl, b, npages * page, width), BF16)
    return pl.pallas_call(
        kern,
        grid_spec=pltpu.PrefetchScalarGridSpec(
            num_scalar_prefetch=1, grid=(nl, b, npages // pps),
            in_specs=[in_spec(q) for q in range(pps)] * 2, out_specs=[out_spec, out_spec]),
        out_shape=[out_sds, out_sds],
        compiler_params=_cparams("parallel", "parallel", "parallel"),
        name="gather_pages",
    )(page_table, *([ck] * pps + [cv] * pps))


def _sb_paged(qn3, kn3, proj3, v_blk0, bias, past_k, past_v, layer, *, nh):
    b, t, width = qn3.shape
    plen = past_k.shape[2]
    sk = LANES
    kb = _pick(plen, 512)
    nsub = kb // sk
    nblk = plen // kb
    ht = nh * t
    scale = LANES ** -0.5
    bias_rows = jnp.broadcast_to(jnp.repeat(bias, t)[:, None], (ht, sk)).astype(F32)
    assert v_blk0 % nh == 0 and sk >= t and plen % kb == 0 and kb % sk == 0

    def kern(q_ref, kn_ref, vn_ref, bias_ref, pk_ref, pv_ref, o_ref, acc_ref, r_ref, qb_ref):
        p = pl.program_id(1)
        u = (_iota2((sk, sk), 0) > _iota2((sk, sk), 1)).astype(BF16)

        def process(ks, v, mask):
            qb = qb_ref[...]
            z = [_dot_nt(qb, k) * scale + bias_ref[...] for k in ks]
            sp, rsum, suf = _sb_logs(z, [mask] * len(ks), u)
            r_run = r_ref[...]
            a = [None] * len(ks)
            for s in range(len(ks) - 1, -1, -1):
                a[s] = _bf(_sb_weight(z[s], sp[s], suf[s], r_run, mask))
                r_run = r_run + rsum[s]
            acc_ref[...] += _dot(a[0] if len(ks) == 1 else jnp.concatenate(a, axis=1), v)
            r_ref[...] = r_run

        @pl.when(p == 0)
        def _():
            blockmask = (_iota2((ht, width), 0) // t) == (_iota2((ht, width), 1) // LANES)
            qt = jnp.concatenate([q_ref[...]] * nh, axis=0)
            qb_ref[...] = jnp.where(blockmask, qt, 0.0).astype(BF16)
            acc_ref[...] = jnp.zeros_like(acc_ref)
            r_ref[...] = jnp.zeros_like(r_ref)
            pad = jnp.zeros((sk - t, width), F32)
            kpad = _bf(jnp.concatenate([kn_ref[...], pad], axis=0))
            vpad = _bf(jnp.concatenate([vn_ref[...], pad], axis=0))
            mask = _iota2((ht, sk), 1) < (_iota2((ht, sk), 0) % t)
            process([kpad], vpad, mask)

        process([pk_ref[s * sk:(s + 1) * sk, :] for s in range(nsub)], pv_ref[...], None)

        @pl.when(p == nblk - 1)
        def _():
            blockmask = (_iota2((ht, width), 0) // t) == (_iota2((ht, width), 1) // LANES)
            accm = jnp.where(blockmask, acc_ref[...], 0.0)
            out = accm[0:t, :]
            for hh in range(1, nh):
                out = out + accm[hh * t:(hh + 1) * t, :]
            o_ref[...] = out.astype(o_ref.dtype)

    past_spec = pl.BlockSpec((None, None, kb, width), lambda i, p: (layer, i, nblk - 1 - p, 0))
    return pl.pallas_call(
        kern,
        grid=(b, nblk),
        in_specs=[pl.BlockSpec((None, t, width), lambda i, p: (i, 0, 0)),
                  pl.BlockSpec((None, t, width), lambda i, p: (i, 0, 0)),
                  pl.BlockSpec((None, t, width), lambda i, p: (i, 0, v_blk0 // nh)),
                  pl.BlockSpec((ht, sk), lambda i, p: (0, 0)),
                  past_spec, past_spec],
        out_specs=pl.BlockSpec((None, t, width), lambda i, p: (i, 0, 0)),
        out_shape=jax.ShapeDtypeStruct((b, t, width), _act_dtype(t)),
        scratch_shapes=[pltpu.VMEM((ht, width), F32), pltpu.VMEM((ht, 1), F32), pltpu.VMEM((ht, width), BF16)],
        compiler_params=_cparams("parallel", "arbitrary"),
        name="sb_paged",
    )(qn3, kn3, proj3, bias_rows, past_k, past_v)


def _ffn_up(xn, w, layer, conv_w, buf, *, b, t):
    m, kdim = xn.shape
    f = w.shape[2] // 2
    kc = conv_w.shape[0]
    tm = _pick(t, 1024, 8)
    tn = _pick(f, 512)
    tpb = t // tm
    nfb = f // tn
    off = 8 - (kc - 1)
    sub = _pick(tm, 256, 8)
    assert m == b * t and t % tm == 0 and w.shape[1] == kdim

    def kern(x_ref, wa_ref, wb_ref, cw_ref, buf_ref, hid_ref, nb_ref, xs, wbf):
        i = pl.program_id(1)

        @pl.when(i == 0)
        def _():
            wbf[:, :tn] = _bf(wa_ref[...])
            wbf[:, tn:] = _bf(wb_ref[...])

        first = (i % tpb) == 0

        @pl.when(first)
        def _():
            xs[off:8, :] = buf_ref[...]

        @pl.when(jnp.logical_not(first))
        def _():
            xs[off:8, :] = xs[tm + off:tm + 8, :]

        def dots(s):
            ab = _dot(x_ref[s * sub:(s + 1) * sub, :], wbf[...])
            return ab[:, :tn], ab[:, tn:]

        def activate(s, a, gate):
            r0 = s * sub
            xs[8 + r0:8 + r0 + sub, :] = a
            y = xs[off + r0:off + r0 + sub, :] * cw_ref[0:1, :]
            for kk in range(1, kc):
                y = y + xs[off + kk + r0:off + kk + r0 + sub, :] * cw_ref[kk:kk + 1, :]
            hid_ref[r0:r0 + sub, :] = (_gelu_tanh(y) * gate).astype(hid_ref.dtype)

        nsub = tm // sub
        cur = dots(0)
        for s in range(1, nsub):
            nxt = dots(s)
            activate(s - 1, *cur)
            cur = nxt
        activate(nsub - 1, *cur)
        nb_ref[...] = xs[tm + off:tm + 8, :]

    return pl.pallas_call(
        kern,
        grid=(nfb, m // tm),
        in_specs=[pl.BlockSpec((tm, kdim), lambda j, i: (i, 0)),
                  pl.BlockSpec((None, kdim, tn), lambda j, i: (layer, 0, j)),
                  pl.BlockSpec((None, kdim, tn), lambda j, i: (layer, 0, nfb + j)),
                  pl.BlockSpec((kc, tn), lambda j, i: (0, j)),
                  pl.BlockSpec((None, kc - 1, tn), lambda j, i: (i // tpb, 0, j))],
        out_specs=[pl.BlockSpec((tm, tn), lambda j, i: (i, j)),
                   pl.BlockSpec((None, kc - 1, tn), lambda j, i: (i // tpb, 0, j))],
        out_shape=[jax.ShapeDtypeStruct((m, f), BF16), jax.ShapeDtypeStruct((b, kc - 1, f), F32)],
        scratch_shapes=[pltpu.VMEM((tm + 8, tn), F32), pltpu.VMEM((kdim, 2 * tn), BF16)],
        compiler_params=_cparams("arbitrary", "arbitrary"),
        name="ffn_up",
    )(xn, w, w, conv_w, buf)


def _ffn_act(u3, buf, conv_w):
    b, t, two_f = u3.shape
    f = two_f // 2
    kc = conv_w.shape[0]
    tc = _pick(f, 8192)
    nfb = f // tc
    off = 8 - (kc - 1)

    def kern(a_ref, b_ref, buf_ref, w_ref, hid_ref, nb_ref, xs):
        xs[off:8, :] = buf_ref[...]
        xs[8:8 + t, :] = a_ref[...]
        nb_ref[...] = xs[t + off:t + 8, :]
        y = xs[off:off + t, :] * w_ref[0:1, :]
        for kk in range(1, kc):
            y = y + xs[off + kk:off + kk + t, :] * w_ref[kk:kk + 1, :]
        hid_ref[...] = _gelu_tanh(y) * b_ref[...]

    return pl.pallas_call(
        kern,
        grid=(b, nfb),
        in_specs=[pl.BlockSpec((None, t, tc), lambda i, j: (i, 0, j)),
                  pl.BlockSpec((None, t, tc), lambda i, j: (i, 0, nfb + j)),
                  pl.BlockSpec((None, kc - 1, tc), lambda i, j: (i, 0, j)),
                  pl.BlockSpec((kc, tc), lambda i, j: (0, j))],
        out_specs=[pl.BlockSpec((None, t, tc), lambda i, j: (i, 0, j)),
                   pl.BlockSpec((None, kc - 1, tc), lambda i, j: (i, 0, j))],
        out_shape=[jax.ShapeDtypeStruct((b, t, f), F32), jax.ShapeDtypeStruct((b, kc - 1, f), F32)],
        scratch_shapes=[pltpu.VMEM((t + 8, tc), F32)],
        compiler_params=_cparams("parallel", "parallel"),
        name="ffn_act",
    )(u3, u3, buf, conv_w)


def _rot_tables(pos, dk):
    half = dk // 2
    inv = 1.0 / (RET_THETA ** (jnp.arange(half, dtype=F32) / half))
    ang = pos.astype(F32)[:, None] * inv[None, :]
    cos, sin = jnp.cos(ang), jnp.sin(ang)
    return jnp.concatenate([cos, cos], -1), jnp.concatenate([-sin, sin], -1)


def _trunk(x3, p4, pos, conv_a, rec_a, rec_b, rec_d, ffn_buf, W, dims, paged=None):
    b, t, d = x3.shape
    m = b * t
    (h_a, h_b, dv_b, h_c, h_d, dv_d, r_d) = dims
    qa = h_a * LANES
    conv_ch = 3 * qa
    qb = h_b * LANES
    vb = h_b * dv_b
    qc = h_c * LANES
    qd = h_d * LANES
    vd = h_d * dv_d
    depth = W["norm_mix"].shape[0]
    h2 = x3.reshape(m, d)
    cos_t, sin_t = _rot_tables(pos, LANES)
    log_gamma = jnp.log1p(-jnp.exp2(-5.0 - jnp.arange(h_b, dtype=F32)))
    ld_ret = jnp.broadcast_to(jnp.repeat(log_gamma, LANES)[None, None, :], (b, t, qb))
    conv_l, ra_l, rb_l, k_l, v_l, rd_l, ffn_l = [], [], [], [], [], [], []

    for i in range(depth):
        j = i // 2
        xn = _rms_cast(h2, W["norm_mix"][i])
        if i % 2 == 0:
            pre3 = _matmul(xn, W["w_in_even_pre"], j).reshape(b, t, conv_ch)
            proj3 = _matmul(xn, W["w_in_even_tail"], j).reshape(b, t, -1)
            sp3 = _matmul(xn, W["w_in_even_small"], j, tn_pref=LANES).reshape(b, t, LANES)
            qkv3, c_new = _conv_a_prep(pre3, conv_a[j], W["conv_a_w"][j], qa=qa, dk=LANES)
            o_a, sa_new = _gated_delta(qkv3, sp3, proj3, 0, W["alog_p"][j], W["dtb_p"][j],
                                       W["a_out_norm"][j].reshape(1, LANES), rec_a[j], nh=h_a)
            off_qb = qa
            o_b, sb_new = _gla(proj3, off_qb // LANES, (off_qb + qb) // LANES, (off_qb + 2 * qb) // dv_b,
                               (off_qb + 2 * qb + vb) // dv_b, W["b_gn"][j].reshape(h_b, 1, dv_b), rec_b[j],
                               nh=h_b, dv=dv_b, k_scale=LANES ** -0.5, norm="group",
                               rot=(cos_t, sin_t), ld3=ld_ret)
            h2 = _matmul_pair(o_a.astype(BF16).reshape(m, -1), o_b.astype(BF16).reshape(m, -1),
                              W["w_out_even"], j, h2)
            conv_l.append(c_new)
            ra_l.append(sa_new)
            rb_l.append(sb_new)
        else:
            proj3 = _matmul(xn, W["w_in_odd_main"], j).reshape(b, t, -1)
            sp3 = _matmul(xn, W["w_in_odd_small"], j, tn_pref=LANES).reshape(b, t, LANES)
            qn3 = _head_rms(proj3, 0, W["c_q_norm"][j], nh=h_c)
            kn3 = _head_rms(proj3, h_c, W["c_k_norm"][j], nh=h_c)
            if paged is None:
                o_c = _sb_prompt(qn3, kn3, proj3, 2 * h_c, W["c_logit_bias"][j], nh=h_c)
            else:
                past_k, past_v = paged
                o_c = _sb_paged(qn3, kn3, proj3, 2 * h_c, W["c_logit_bias"][j], past_k, past_v, j, nh=h_c)
            off_qd = 3 * qc
            o_d, sd_new = _gla(proj3, off_qd // LANES, (off_qd + qd) // LANES, (off_qd + 2 * qd) // dv_d,
                               (off_qd + 2 * qd + vd) // dv_d, W["d_gain"][j], rec_d[j],
                               nh=h_d, dv=dv_d, k_scale=LANES ** -0.5, norm="rms",
                               lowrank=(sp3, W["d_w2p"][j], W["d_alpha_b"][j].reshape(1, qd)))
            h2 = _matmul_pair(o_c.astype(BF16).reshape(m, -1), o_d.astype(BF16).reshape(m, -1),
                              W["w_out_odd"], j, h2)
            k_l.append(kn3.reshape(b, t, h_c, LANES))
            v_l.append(proj3[:, :, 2 * qc:3 * qc].reshape(b, t, h_c, LANES))
            rd_l.append(sd_new)
        xn = _rms_cast(h2, W["norm_ffn"][i])
        if t % BF16_SUBLANES == 0:
            hid, fb = _ffn_up(xn, W["w_ffn_in"], i, W["ffn_conv_w"][i], ffn_buf[i], b=b, t=t)
        else:
            u3 = _matmul(xn, W["w_ffn_in"], i).reshape(b, t, -1)
            hid, fb = _ffn_act(u3, ffn_buf[i], W["ffn_conv_w"][i])
            hid = hid.astype(BF16).reshape(m, -1)
        f = hid.shape[-1]
        tk_f = f // 2 if (f // 2) % LANES == 0 else f
        h2 = _matmul(hid, W["w_ffn_out"], i, res=h2, tk=tk_f)
        ffn_l.append(fb)
        xn = _rms_cast(h2, W["ple_norm"][i])
        h2 = _matmul(xn, W["w_ple_gate"], i, ple=(h2, p4[i].reshape(m, -1).astype(BF16), W["w_ple_in"]))
    return (h2.reshape(b, t, d), jnp.stack(conv_l), jnp.stack(ra_l), jnp.stack(rb_l), jnp.stack(k_l),
            jnp.stack(v_l), jnp.stack(rd_l), jnp.stack(ffn_l))


def kernel(x_prompt, x_sample, state_a_conv, state_a_rec, state_b_rec, cache_k, cache_v, state_d_rec, state_ffn_conv, page_table, p_prompt, p_sample, norm_mix, w_in_even, conv_a_w, a_log, dt_bias, a_out_norm, b_gn, w_out_even, w_in_odd, c_q_norm, c_k_norm, c_logit_bias, d_alpha_w2, d_alpha_b, d_out_norm, w_out_odd, norm_ffn, w_ffn_in, ffn_conv_w, w_ffn_out, ple_norm, w_ple_gate, w_ple_in):
    n_even, h_a = a_log.shape
    n_odd, h_c = c_logit_bias.shape
    h_b, dv_b = state_b_rec.shape[2], state_b_rec.shape[4]
    h_d, dv_d = state_d_rec.shape[2], state_d_rec.shape[4]
    r_d = d_alpha_w2.shape[1]
    depth = norm_mix.shape[0]
    conv_ch = conv_a_w.shape[2]
    qa = h_a * LANES
    qc = h_c * LANES
    qd = h_d * LANES
    assert conv_ch == 3 * qa and state_a_rec.shape[3:] == (LANES, LANES)
    assert state_b_rec.shape[3] == LANES and state_d_rec.shape[3] == LANES and cache_k.shape[4] == LANES
    assert 2 * h_a <= LANES and r_d <= LANES
    dims = (h_a, h_b, dv_b, h_c, h_d, dv_d, r_d)

    w_tail_e = w_in_even[:, :, conv_ch + 2 * h_a:].astype(BF16)
    w_small_e = jnp.pad(w_in_even[:, :, conv_ch:conv_ch + 2 * h_a],
                        ((0, 0), (0, 0), (0, LANES - 2 * h_a))).astype(BF16)
    n_main_o = 3 * qc + 2 * qd + 2 * h_d * dv_d
    w_small_o = jnp.pad(w_in_odd[:, :, n_main_o:], ((0, 0), (0, 0), (0, LANES - r_d))).astype(BF16)
    pad_ab = lambda v: jnp.pad(v, ((0, 0), (h_a, LANES - 2 * h_a)))[:, None, :]
    W = {
        "norm_mix": norm_mix, "norm_ffn": norm_ffn, "ple_norm": ple_norm,
        "w_in_even_pre": w_in_even[:, :, :conv_ch].astype(BF16), "w_in_even_tail": w_tail_e,
        "w_in_even_small": w_small_e,
        "w_in_odd_main": w_in_odd[:, :, :n_main_o].astype(BF16), "w_in_odd_small": w_small_o,
        "conv_a_w": conv_a_w, "alog_p": pad_ab(a_log), "dtb_p": pad_ab(dt_bias),
        "a_out_norm": a_out_norm, "b_gn": b_gn,
        "w_out_even": w_out_even, "w_out_odd": w_out_odd,
        "c_q_norm": c_q_norm, "c_k_norm": c_k_norm, "c_logit_bias": c_logit_bias,
        "d_w2p": jnp.pad(d_alpha_w2, ((0, 0), (0, LANES - r_d), (0, 0))), "d_alpha_b": d_alpha_b,
        "d_gain": jnp.broadcast_to(d_out_norm[:, None, None, :], (n_odd, h_d, 1, dv_d)),
        "w_ffn_in": w_ffn_in, "ffn_conv_w": ffn_conv_w, "w_ffn_out": w_ffn_out.astype(BF16),
        "w_ple_gate": w_ple_gate, "w_ple_in": w_ple_in.astype(BF16),
    }

    bp, tp = x_prompt.shape[0], x_prompt.shape[1]
    kconv = conv_a_w.shape[1]
    kffn = ffn_conv_w.shape[1]
    f = w_ffn_out.shape[1]
    zeros = lambda *s: jnp.zeros(s, F32)
    out_p = _trunk(x_prompt, p_prompt, jnp.arange(tp, dtype=jnp.int32),
                   zeros(n_even, bp, kconv - 1, conv_ch), zeros(n_even, bp, h_a, LANES, LANES),
                   zeros(n_even, bp, h_b, LANES, dv_b), zeros(n_odd, bp, h_d, LANES, dv_d),
                   zeros(depth, bp, kffn - 1, f), W, dims)

    ts = x_sample.shape[1]
    past_len = page_table.shape[1] * cache_k.shape[2]
    pos_s = past_len + jnp.arange(ts, dtype=jnp.int32)
    out_s = _trunk(x_sample, p_sample, pos_s, state_a_conv, state_a_rec, state_b_rec, state_d_rec,
                   state_ffn_conv, W, dims, paged=_gather_pages(cache_k, cache_v, page_table))

    res = []
    for a, s in zip(out_p, out_s):
        res += [a, s]
    return tuple(res)
```

```python
import math

import jax
import jax.numpy as jnp
from jax import lax
from jax.experimental import pallas as pl
from jax.experimental.pallas import tpu as pltpu

F32 = jnp.float32
BF16 = jnp.bfloat16
EPS = 1e-6
LANES = 128
BF16_SUBLANES = 16
CHUNK = 64
SUB = 16
RET_THETA = 10000.0
TAU_D = 16.0
VMEM_LIMIT = 56 * 1024 * 1024


def _cparams(*sem):
    return pltpu.CompilerParams(dimension_semantics=sem, vmem_limit_bytes=VMEM_LIMIT)


def _pick(n, pref, mult=LANES):
    t = (min(pref, n) // mult) * mult
    while t >= mult:
        if n % t == 0:
            return t
        t -= mult
    return n


def _act_dtype(t):
    return BF16 if t % BF16_SUBLANES == 0 else F32


def _dot(a, b):
    return jnp.dot(a, b, preferred_element_type=F32)


def _dot_nt(a, b):
    return lax.dot_general(a, b, (((1,), (1,)), ((), ())), preferred_element_type=F32)


def _dot_tn(a, b):
    return lax.dot_general(a, b, (((0,), (0,)), ((), ())), preferred_element_type=F32)


def _bf(x):
    return x.astype(BF16)


def _silu(x):
    return x * jax.nn.sigmoid(x)


def _softplus(x):
    return jnp.maximum(x, 0.0) + jnp.log1p(jnp.exp(-jnp.abs(x)))


def _log_sigmoid(x):
    return jnp.minimum(x, 0.0) - jnp.log1p(jnp.exp(-jnp.abs(x)))


def _gelu_tanh(x):
    c = math.sqrt(2.0 / math.pi)
    return 0.5 * x * (1.0 + jnp.tanh(c * (x + 0.044715 * (x * x * x))))


def _iota2(shape, dim):
    return lax.broadcasted_iota(jnp.int32, shape, dim)


def _split3(x):
    hi = x.astype(BF16)
    r1 = x - hi.astype(F32)
    mid = r1.astype(BF16)
    lo = (r1 - mid.astype(F32)).astype(BF16)
    return hi, mid, lo


def _dot01(m01, x):
    hi, mid, lo = _split3(x)
    return _dot(m01, hi) + _dot(m01, mid) + _dot(m01, lo)


def _rms_cast(x2, g):
    m, d = x2.shape
    tm = _pick(m, 512, 8)

    def kern(x_ref, g_ref, o_ref):
        x = x_ref[...]
        r = lax.rsqrt(jnp.mean(x * x, axis=-1, keepdims=True) + EPS)
        o_ref[...] = (x * r * g_ref[...]).astype(o_ref.dtype)

    return pl.pallas_call(
        kern,
        grid=(m // tm,),
        in_specs=[pl.BlockSpec((tm, d), lambda i: (i, 0)), pl.BlockSpec((1, d), lambda i: (0, 0))],
        out_specs=pl.BlockSpec((tm, d), lambda i: (i, 0)),
        out_shape=jax.ShapeDtypeStruct((m, d), BF16),
        compiler_params=_cparams("parallel"),
        name="rms_cast",
    )(x2, g.reshape(1, d))


def _matmul(x, w, layer, *, res=None, ple=None, tm_pref=1024, tn_pref=512, tk=None):
    m, kdim = x.shape
    n = w.shape[2]
    tm = _pick(m, tm_pref, 8)
    tn = _pick(n, tn_pref)
    tk = kdim if tk is None else tk
    nk = kdim // tk
    assert kdim % tk == 0 and m % tm == 0 and n % tn == 0 and w.shape[1] == kdim

    in_specs = [pl.BlockSpec((tm, tk), lambda i, j, k: (i, k)),
                pl.BlockSpec((None, tk, tn), lambda i, j, k: (layer, k, j))]
    args = [x, w]
    if res is not None:
        in_specs.append(pl.BlockSpec((tm, tn), lambda i, j, k: (i, j)))
        args.append(res)
    if ple is not None:
        h, p, w_in = ple
        pdim = p.shape[1]
        in_specs += [pl.BlockSpec((tm, tn), lambda i, j, k: (i, j)),
                     pl.BlockSpec((tm, pdim), lambda i, j, k: (i, 0)),
                     pl.BlockSpec((None, pdim, tn), lambda i, j, k: (layer, 0, j))]
        args += [h, p, w_in]

    def kern(*refs):
        x_ref, w_ref = refs[0], refs[1]
        o_ref = refs[len(args)]
        acc_ref = refs[len(args) + 1] if nk > 1 else None

        def epilogue(acc):
            if res is not None:
                return refs[2][...] + acc
            if ple is not None:
                h_ref, p_ref, wi_ref = refs[2], refs[3], refs[4]
                emb = _dot(p_ref[...], wi_ref[...])
                return h_ref[...] + jax.nn.sigmoid(acc) * emb
            return acc

        part = _dot(x_ref[...], _bf(w_ref[...]))
        if nk == 1:
            o_ref[...] = epilogue(part)
        else:
            k = pl.program_id(2)

            @pl.when(k == 0)
            def _():
                acc_ref[...] = part

            @pl.when(jnp.logical_and(k > 0, k < nk - 1))
            def _():
                acc_ref[...] += part

            @pl.when(k == nk - 1)
            def _():
                o_ref[...] = epilogue(acc_ref[...] + part)

    return pl.pallas_call(
        kern,
        grid=(m // tm, n // tn, nk),
        in_specs=in_specs,
        out_specs=pl.BlockSpec((tm, tn), lambda i, j, k: (i, j)),
        out_shape=jax.ShapeDtypeStruct((m, n), F32),
        scratch_shapes=[pltpu.VMEM((tm, tn), F32)] if nk > 1 else [],
        compiler_params=_cparams("parallel", "parallel", "arbitrary"),
        name="matmul",
    )(*args)


def _matmul_pair(xa, xb, w, layer, res):
    m, kh = xa.shape
    n = w.shape[2]
    tm = _pick(m, 1024, 8)
    tn = _pick(n, 512)
    assert xb.shape == (m, kh) and w.shape[1] == 2 * kh

    def kern(xa_ref, xb_ref, wa_ref, wb_ref, res_ref, o_ref):
        o_ref[...] = res_ref[...] + (_dot(xa_ref[...], _bf(wa_ref[...])) + _dot(xb_ref[...], _bf(wb_ref[...])))

    return pl.pallas_call(
        kern,
        grid=(m // tm, n // tn),
        in_specs=[pl.BlockSpec((tm, kh), lambda i, j: (i, 0)), pl.BlockSpec((tm, kh), lambda i, j: (i, 0)),
                  pl.BlockSpec((None, kh, tn), lambda i, j: (layer, 0, j)),
                  pl.BlockSpec((None, kh, tn), lambda i, j: (layer, 1, j)),
                  pl.BlockSpec((tm, tn), lambda i, j: (i, j))],
        out_specs=pl.BlockSpec((tm, tn), lambda i, j: (i, j)),
        out_shape=jax.ShapeDtypeStruct((m, n), F32),
        compiler_params=_cparams("parallel", "parallel"),
        name="matmul_pair",
    )(xa, xb, w, w, res)


def _load_chunk(ref, r0, t):
    if t >= CHUNK:
        return ref[pl.ds(r0, CHUNK), :]
    x = ref[...]
    return jnp.concatenate([x, jnp.zeros((CHUNK - t, x.shape[1]), x.dtype)], axis=0)


def _store_chunk(ref, r0, t, lanes, val):
    if t >= CHUNK:
        ref[pl.ds(r0, CHUNK), lanes] = val.astype(ref.dtype)
    else:
        ref[:, lanes] = val[:t].astype(ref.dtype)


def _time_tile(t):
    tt = min(t, 512)
    assert t % tt == 0 and (tt % CHUNK == 0 or t < CHUNK)
    return tt


def _inv_unit_lower(a, eye, same_sub, nb):
    a_d = [jnp.where(same_sub, x, 0.0) for x in a]
    low = [x - y for x, y in zip(a, a_d)]
    dinv = [eye - x for x in a_d]
    pb = [_bf(-x) for x in a_d]
    s = 2
    while s < SUB:
        pb = [_bf(_dot(x, x)) for x in pb]
        dinv = [d + _dot(_bf(d), x) for d, x in zip(dinv, pb)]
        s *= 2
    if nb == 1:
        return dinv
    db = [_bf(d) for d in dinv]
    nmat = [_dot(d, _bf(x)) for d, x in zip(db, low)]
    nb16 = [_bf(x) for x in nmat]
    acc = [eye - x for x in nmat]
    pw = nb16
    sign = -1.0
    for _ in range(2, nb):
        pwm = [_dot(x, y) for x, y in zip(pw, nb16)]
        pw = [_bf(x) for x in pwm]
        sign = -sign
        acc = [x + sign * y for x, y in zip(acc, pwm)]
    return [_dot(_bf(x), d) for x, d in zip(acc, db)]


def _conv_a_prep(proj3, buf, conv_w, *, qa, dk):
    b, t, _ = proj3.shape
    kc, ch = conv_w.shape
    tc = _pick(qa, 512)
    nqb = qa // tc
    assert ch % tc == 0
    off = 8 - (kc - 1)

    def kern(x_ref, buf_ref, w_ref, o_ref, cn_ref, xs):
        j = pl.program_id(1)
        xs[off:8, :] = buf_ref[...]
        xs[8:8 + t, :] = x_ref[...]
        cn_ref[...] = xs[t + off:t + 8, :]
        is_q = j < nqb
        is_k = j < 2 * nqb
        scale = jnp.where(is_q, dk ** -0.5, 1.0).astype(F32)
        for hh in range(tc // LANES):
            sl = slice(hh * LANES, (hh + 1) * LANES)
            y = xs[off:off + t, sl] * w_ref[0:1, sl]
            for kk in range(1, kc):
                y = y + xs[off + kk:off + kk + t, sl] * w_ref[kk:kk + 1, sl]
            s = _silu(y)
            nrm = s * lax.rsqrt(jnp.sum(s * s, axis=-1, keepdims=True) + EPS) * scale
            o_ref[:, sl] = jnp.where(is_k, nrm, s)

    return pl.pallas_call(
        kern,
        grid=(b, ch // tc),
        in_specs=[pl.BlockSpec((None, t, tc), lambda i, j: (i, 0, j)),
                  pl.BlockSpec((None, kc - 1, tc), lambda i, j: (i, 0, j)),
                  pl.BlockSpec((kc, tc), lambda i, j: (0, j))],
        out_specs=[pl.BlockSpec((None, t, tc), lambda i, j: (i, 0, j)),
                   pl.BlockSpec((None, kc - 1, tc), lambda i, j: (i, 0, j))],
        out_shape=[jax.ShapeDtypeStruct((b, t, ch), F32), jax.ShapeDtypeStruct((b, kc - 1, ch), F32)],
        scratch_shapes=[pltpu.VMEM((t + 8, tc), F32)],
        compiler_params=_cparams("parallel", "parallel"),
        name="conv_a_prep",
    )(proj3, buf, conv_w)


def _gated_delta(qkv3, sp3, proj3, gate_blk0, alog_p, dtb_p, a_norm, s0, *, nh):
    b, t, _ = qkv3.shape
    c = CHUNK
    hs = min(8, nh)
    gh = min(2, hs)
    gs = hs // gh
    n = gh * c
    tt = _time_tile(t)
    nchunks = max(tt // c, 1)
    wl = hs * LANES
    assert nh % hs == 0 and hs % gh == 0 and gate_blk0 % hs == 0
    odt = _act_dtype(t)

    def kern(q_ref, k_ref, v_ref, sp_ref, gate_ref, alog_ref, dtb_ref, an_ref, s0_ref, o_ref, s_ref):
        hb = pl.program_id(1)
        ti = pl.program_id(2)

        @pl.when(ti == 0)
        def _():
            s_ref[...] = s0_ref[...]

        row = _iota2((n, n), 0)
        col = _iota2((n, n), 1)
        same = (row // c) == (col // c)
        incl = jnp.logical_and(same, col <= row)
        strict = jnp.logical_and(same, col < row)
        same_sub = (row // SUB) == (col // SUB)
        lbd = incl.astype(BF16)
        eye = (row == col).astype(F32)
        lane = _iota2((n, LANES), 1)
        rown = _iota2((n, 1), 0)
        valid = (rown % c) < t
        neg_a = -jnp.exp(alog_ref[...])
        dtb = dtb_ref[...]
        an = an_ref[...]

        def stack(x, g0):
            return jnp.concatenate([x[:, (g0 + i) * LANES:(g0 + i + 1) * LANES] for i in range(gh)], axis=0)

        def chunk(ci, carry):
            r0 = pl.multiple_of(ci * c, c)
            qc = _load_chunk(q_ref, r0, tt)
            kc = _load_chunk(k_ref, r0, tt)
            vc = _load_chunk(v_ref, r0, tt)
            gc = _load_chunk(gate_ref, r0, tt)
            sp = _load_chunk(sp_ref, r0, tt)
            beta_all = jax.nn.sigmoid(sp)
            g_all = neg_a * _softplus(sp + dtb)
            bt = jnp.concatenate([beta_all] * gh, axis=0)
            gt = jnp.concatenate([g_all] * gh, axis=0)
            grp = range(gs)
            rowhead = [hb * hs + gi * gh + rown // c for gi in grp]
            q = [stack(qc, gi * gh) for gi in grp]
            k = [stack(kc, gi * gh) for gi in grp]
            v = [stack(vc, gi * gh) for gi in grp]
            s_old = [[s_ref[gi * gh + i] for i in range(gh)] for gi in grp]
            bcol = [jnp.sum(jnp.where(jnp.logical_and(lane == rh, valid), bt, 0.0), axis=1, keepdims=True)
                    for rh in rowhead]
            g_m = [jnp.where(jnp.logical_and(lane == nh + rh, valid), gt, 0.0) for rh in rowhead]
            gcol = [jnp.sum(_dot01(lbd, x), axis=1, keepdims=True) for x in g_m]
            grow = [jnp.sum(eye * x, axis=0, keepdims=True) for x in gcol]
            decay = [jnp.where(incl, jnp.exp(jnp.where(incl, x - y, 0.0)), 0.0) for x, y in zip(gcol, grow)]
            eg = [jnp.exp(x) for x in gcol]
            kb = [_bf(x) for x in k]
            kk = [_dot_nt(x, x) for x in kb]
            amat = [jnp.where(strict, x * d * bc, 0.0) for x, d, bc in zip(kk, decay, bcol)]
            inv = _inv_unit_lower(amat, eye, same_sub, c // SUB)
            rhs = [jnp.concatenate([vv * bc, kx * (bc * e)], axis=1)
                   for vv, kx, bc, e in zip(v, k, bcol, eg)]
            sol = [_dot(_bf(x), _bf(y)) for x, y in zip(inv, rhs)]
            qe = [x * e for x, e in zip(q, eg)]
            sob = [[_bf(x) for x in row_] for row_ in s_old]
            u = [jnp.concatenate([sol[gi][i * c:(i + 1) * c, :LANES]
                                  - _dot(_bf(sol[gi][i * c:(i + 1) * c, LANES:]), sob[gi][i])
                                  for i in range(gh)], axis=0) for gi in grp]
            oi = [jnp.concatenate([_dot(_bf(qe[gi][i * c:(i + 1) * c]), sob[gi][i]) for i in range(gh)], axis=0)
                  for gi in grp]
            ub = [_bf(x) for x in u]
            attn = [_dot_nt(_bf(x), y) * d for x, y, d in zip(q, kb, decay)]
            o = [x + _dot(_bf(y), z) for x, y, z in zip(oi, attn, ub)]
            on = [x * lax.rsqrt(jnp.mean(x * x, axis=-1, keepdims=True) + EPS) * an for x in o]
            for gi in grp:
                for i in range(gh):
                    rs = slice(i * c, (i + 1) * c)
                    lanes = slice((gi * gh + i) * LANES, (gi * gh + i + 1) * LANES)
                    g_last = gcol[gi][(i + 1) * c - 1:(i + 1) * c, :]
                    k_dec = k[gi][rs] * jnp.exp(g_last - gcol[gi][rs])
                    s_ref[gi * gh + i] = s_old[gi][i] * jnp.exp(g_last) + _dot_tn(_bf(k_dec), ub[gi][rs])
                    _store_chunk(o_ref, r0, tt, lanes, on[gi][rs] * _silu(gc[:, lanes]))
            return carry

        if nchunks == 1:
            chunk(0, 0)
        else:
            lax.fori_loop(0, nchunks, chunk, 0)

    blk = lambda off: pl.BlockSpec((None, tt, wl), lambda i, j, ti: (i, ti, off + j))
    vec = pl.BlockSpec((1, LANES), lambda i, j, ti: (0, 0))
    st = pl.BlockSpec((None, hs, LANES, LANES), lambda i, j, ti: (i, j, 0, 0))
    return pl.pallas_call(
        kern,
        grid=(b, nh // hs, t // tt),
        in_specs=[blk(0), blk(nh // hs), blk(2 * nh // hs),
                  pl.BlockSpec((None, tt, LANES), lambda i, j, ti: (i, ti, 0)),
                  blk(gate_blk0 // hs), vec, vec, vec, st],
        out_specs=[blk(0), st],
        out_shape=[jax.ShapeDtypeStruct((b, t, nh * LANES), odt),
                   jax.ShapeDtypeStruct((b, nh, LANES, LANES), F32)],
        compiler_params=_cparams("parallel", "parallel", "arbitrary"),
        name="gated_delta",
    )(qkv3, qkv3, qkv3, sp3, proj3, alog_p, dtb_p, a_norm, s0)


def _gla(proj3, q_blk0, k_blk0, v_blk0, gate_blk0, gain, s0, *, nh, dv, k_scale, norm,
         rot=None, ld3=None, lowrank=None):
    b, t, _ = proj3.shape
    c = CHUNK
    sb = SUB
    nsb = c // sb
    dk = LANES
    hg = min(8, nh)
    tt = _time_tile(t)
    nchunks = max(tt // c, 1)
    assert nh % hg == 0 and q_blk0 % hg == 0 and k_blk0 % hg == 0 and v_blk0 % hg == 0 and gate_blk0 % hg == 0
    odt = _act_dtype(t)
    n_in = 5 + (2 if rot is not None else 0) + (1 if ld3 is not None else 0) + (3 if lowrank is not None else 0)

    def kern(*refs):
        q_ref, k_ref, v_ref, gate_ref, gain_ref = refs[:5]
        pos = 5
        if rot is not None:
            cos_ref, sin_ref = refs[pos], refs[pos + 1]
            pos += 2
        if ld3 is not None:
            ld_ref = refs[pos]
            pos += 1
        if lowrank is not None:
            sp_ref, w2_ref, ab_ref = refs[pos:pos + 3]
            pos += 3
        s0_ref = refs[pos]
        o_ref, s_ref = refs[n_in + 1], refs[n_in + 2]
        st_ref = refs[n_in + 3]
        ti = pl.program_id(2)

        @pl.when(ti == 0)
        def _():
            for hh in range(hg):
                st_ref[hh] = s0_ref[hh].T

        row = _iota2((c, c), 0)
        col = _iota2((c, c), 1)
        lincl = (col <= row).astype(BF16)
        same = (row // sb) == (col // sb)
        diag_mask = jnp.logical_and(same, col <= row)
        below = (col // sb) < (row // sb)
        colmasks = [jnp.logical_and(same, (col % sb) == j) for j in range(sb)]
        rvalid = _iota2((c, 1), 0) < t

        def heads(q, k, v, ld, gate):
            hd = range(hg)
            g = [_dot01(lincl, x) for x in ld]
            st = [st_ref[hh] for hh in hd]
            o = [_dot_nt(_bf(x * jnp.exp(y)), _bf(s)) for x, y, s in zip(q, g, st)]
            offs = [[jnp.zeros((sb, c), F32)] for _ in hd]
            for a in range(1, nsb):
                gs = [x[a * sb - 1:a * sb, :] for x in g]
                qt = [x[a * sb:(a + 1) * sb, :] * jnp.exp(y[a * sb:(a + 1) * sb, :] - z)
                      for x, y, z in zip(q, g, gs)]
                kt = [x * jnp.exp(jnp.minimum(z - y, 0.0)) for x, y, z in zip(k, g, gs)]
                for hh in hd:
                    offs[hh].append(_dot_nt(_bf(qt[hh]), _bf(kt[hh])))
            s_off = [jnp.concatenate(x, axis=0) for x in offs]
            g3 = [x.reshape(nsb, sb, dk) for x in g]
            q3 = [x.reshape(nsb, sb, dk) for x in q]
            k3 = [x.reshape(nsb, sb, dk) for x in k]
            pd = [jnp.zeros((c, c), F32) for _ in hd]
            for j in range(sb):
                e = [jnp.exp(jnp.minimum(x - x[:, j:j + 1, :], 0.0)) for x in g3]
                colj = [jnp.sum(x * y[:, j:j + 1, :] * z, axis=-1, keepdims=True).reshape(c, 1)
                        for x, y, z in zip(q3, k3, e)]
                pd = [jnp.where(colmasks[j], x, y) for x, y in zip(colj, pd)]
            scores = [jnp.where(diag_mask, x, jnp.where(below, y, 0.0)) for x, y in zip(pd, s_off)]
            vb = [_bf(x) for x in v]
            o = [x + _dot(_bf(y), z) for x, y, z in zip(o, scores, vb)]
            g_last = [x[c - 1:c, :] for x in g]
            k_dec = [x * jnp.exp(y - z) for x, y, z in zip(k, g_last, g)]
            for hh in hd:
                st_ref[hh] = st[hh] * jnp.exp(g_last[hh]) + _dot_tn(vb[hh], _bf(k_dec[hh]))
            if norm == "group":
                oc = [x - jnp.mean(x, axis=-1, keepdims=True) for x in o]
                on = [x * lax.rsqrt(jnp.mean(x * x, axis=-1, keepdims=True) + EPS) for x in oc]
            else:
                on = [x * lax.rsqrt(jnp.mean(x * x, axis=-1, keepdims=True) + EPS) for x in o]
            return [on[hh] * gain_ref[hh] * _silu(gate[hh]) for hh in hd]

        def chunk(ci, carry):
            r0 = pl.multiple_of(ci * c, c)
            qc = _load_chunk(q_ref, r0, tt)
            kc = _load_chunk(k_ref, r0, tt)
            vc = _load_chunk(v_ref, r0, tt)
            gatec = _load_chunk(gate_ref, r0, tt)
            if rot is not None:
                cs = _load_chunk(cos_ref, r0, tt)
                sn = _load_chunk(sin_ref, r0, tt)
            if ld3 is not None:
                ldc = _load_chunk(ld_ref, r0, tt)
            else:
                sp = _load_chunk(sp_ref, r0, tt)
                logit = _dot(_bf(sp), _bf(w2_ref[...])) + ab_ref[...]
                ldc = jnp.where(rvalid, _log_sigmoid(logit) / TAU_D, 0.0)
            ks = [slice(hh * dk, (hh + 1) * dk) for hh in range(hg)]
            vs = [slice(hh * dv, (hh + 1) * dv) for hh in range(hg)]
            q = [qc[:, s] for s in ks]
            k = [kc[:, s] for s in ks]
            if rot is not None:
                q = [x * cs + pltpu.roll(x, dk // 2, axis=1) * sn for x in q]
                k = [x * cs + pltpu.roll(x, dk // 2, axis=1) * sn for x in k]
            k = [x * k_scale for x in k]
            out = heads(q, k, [vc[:, s] for s in vs], [ldc[:, s] for s in ks], [gatec[:, s] for s in vs])
            for hh in range(hg):
                _store_chunk(o_ref, r0, tt, vs[hh], out[hh])
            return carry

        if nchunks == 1:
            chunk(0, 0)
        else:
            lax.fori_loop(0, nchunks, chunk, 0)

        @pl.when(ti == pl.num_programs(2) - 1)
        def _():
            for hh in range(hg):
                s_ref[hh] = st_ref[hh].T

    def blk(off, width):
        return pl.BlockSpec((None, tt, hg * width), lambda i, j, ti: (i, ti, off // hg + j))

    in_specs = [blk(q_blk0, dk), blk(k_blk0, dk), blk(v_blk0, dv), blk(gate_blk0, dv),
                pl.BlockSpec((hg, 1, dv), lambda i, j, ti: (j, 0, 0))]
    args = [proj3, proj3, proj3, proj3, gain]
    if rot is not None:
        in_specs += [pl.BlockSpec((tt, dk), lambda i, j, ti: (ti, 0))] * 2
        args += list(rot)
    if ld3 is not None:
        in_specs.append(blk(0, dk))
        args.append(ld3)
    if lowrank is not None:
        sp3, w2p, ab = lowrank
        in_specs += [pl.BlockSpec((None, tt, LANES), lambda i, j, ti: (i, ti, 0)),
                     pl.BlockSpec((LANES, hg * dk), lambda i, j, ti: (0, j)),
                     pl.BlockSpec((1, hg * dk), lambda i, j, ti: (0, j))]
        args += [sp3, w2p, ab]
    st_spec = pl.BlockSpec((None, hg, dk, dv), lambda i, j, ti: (i, j, 0, 0))
    in_specs.append(st_spec)
    args.append(s0)
    assert len(args) == n_in + 1

    return pl.pallas_call(
        kern,
        grid=(b, nh // hg, t // tt),
        in_specs=in_specs,
        out_specs=[blk(0, dv), st_spec],
        out_shape=[jax.ShapeDtypeStruct((b, t, nh * dv), odt),
                   jax.ShapeDtypeStruct((b, nh, dk, dv), F32)],
        scratch_shapes=[pltpu.VMEM((hg, dv, dk), F32)],
        compiler_params=_cparams("parallel", "parallel", "arbitrary"),
        name="gla_" + norm,
    )(*args)


def _head_rms(proj3, blk0, gain, *, nh):
    b, t, _ = proj3.shape
    width = nh * LANES
    tc = _pick(width, 512)
    per = tc // LANES
    c0 = blk0 // per
    assert blk0 % per == 0

    def kern(x_ref, g_ref, o_ref):
        g = g_ref[...]
        for hh in range(per):
            sl = slice(hh * LANES, (hh + 1) * LANES)
            x = x_ref[:, sl]
            o_ref[:, sl] = x * lax.rsqrt(jnp.mean(x * x, axis=-1, keepdims=True) + EPS) * g

    return pl.pallas_call(
        kern,
        grid=(b, width // tc),
        in_specs=[pl.BlockSpec((None, t, tc), lambda i, j: (i, 0, c0 + j)),
                  pl.BlockSpec((1, LANES), lambda i, j: (0, 0))],
        out_specs=pl.BlockSpec((None, t, tc), lambda i, j: (i, 0, j)),
        out_shape=jax.ShapeDtypeStruct((b, t, width), F32),
        compiler_params=_cparams("parallel", "parallel"),
        name="head_rms",
    )(proj3, gain.reshape(1, LANES))


def _sb_logs(z, masks, u):
    sp = [jnp.maximum(x, 0.0) + jnp.log(1.0 + jnp.exp(-jnp.abs(x))) for x in z]
    l1 = [-x if m is None else jnp.where(m, -x, 0.0) for x, m in zip(sp, masks)]
    rsum = [jnp.sum(x, axis=1, keepdims=True) for x in l1]
    suf = [_dot(_bf(x), u) for x in l1]
    return sp, rsum, suf


def _sb_weight(z, sp, suf, r_run, mask):
    w = jnp.exp((z - sp) + (suf + r_run))
    return w if mask is None else jnp.where(mask, w, 0.0)


def _sb_prompt(qn3, kn3, proj3, v_blk0, bias, *, nh):
    b, t, _ = qn3.shape
    bq = min(256, t)
    bk = min(128, t)
    ratio = bq // bk
    hp = min(4, nh)
    assert t % bq == 0 and bq % bk == 0 and nh % hp == 0 and v_blk0 % hp == 0
    scale = LANES ** -0.5

    def kern(bias_ref, q_ref, k_ref, v_ref, o_ref):
        hb = pl.program_id(1)
        qi = pl.program_id(2)
        u = (_iota2((bk, bk), 0) > _iota2((bk, bk), 1)).astype(BF16)
        dmask = [_iota2((bq, bk), 1) + s * bk < _iota2((bq, bk), 0) for s in range(ratio)]
        qs = [_bf(q_ref[:, hh * LANES:(hh + 1) * LANES]) for hh in range(hp)]
        biases = [bias_ref[hb * hp + hh] for hh in range(hp)]

        def step(kblk, carry, masks):
            pairs = [(hh, s) for hh in range(hp) for s in range(ratio - 1, -1, -1)]
            k0 = {s: pl.multiple_of(kblk * bq + s * bk, bk) for s in range(ratio)}
            lanes = [slice(hh * LANES, (hh + 1) * LANES) for hh in range(hp)]
            kk = [_bf(k_ref[pl.ds(k0[s], bk), lanes[hh]]) for hh, s in pairs]
            vv = [_bf(v_ref[pl.ds(k0[s], bk), lanes[hh]]) for hh, s in pairs]
            z = [_dot_nt(qs[hh], x) * scale + biases[hh] for (hh, s), x in zip(pairs, kk)]
            pmask = [None if masks is None else masks[s] for hh, s in pairs]
            sp, rsum, suf = _sb_logs(z, pmask, u)
            new = []
            for hh in range(hp):
                r_run, acc = carry[hh]
                for idx, (h2, s) in enumerate(pairs):
                    if h2 != hh:
                        continue
                    a = _sb_weight(z[idx], sp[idx], suf[idx], r_run, pmask[idx])
                    acc = acc + _dot(_bf(a), vv[idx])
                    r_run = r_run + rsum[idx]
                new.append((r_run, acc))
            return tuple(new)

        init = tuple((jnp.zeros((bq, 1), F32), jnp.zeros((bq, LANES), F32)) for _ in range(hp))
        carry = step(qi, init, dmask)
        carry = lax.fori_loop(0, qi, lambda i, cr: step(qi - 1 - i, cr, None), carry)
        for hh in range(hp):
            o_ref[:, hh * LANES:(hh + 1) * LANES] = carry[hh][1].astype(o_ref.dtype)

    wl = hp * LANES
    return pl.pallas_call(
        kern,
        grid=(b, nh // hp, t // bq),
        in_specs=[pl.BlockSpec(memory_space=pltpu.SMEM),
                  pl.BlockSpec((None, bq, wl), lambda i, j, qq: (i, qq, j)),
                  pl.BlockSpec((None, t, wl), lambda i, j, qq: (i, 0, j)),
                  pl.BlockSpec((None, t, wl), lambda i, j, qq: (i, 0, v_blk0 // hp + j))],
        out_specs=pl.BlockSpec((None, bq, wl), lambda i, j, qq: (i, qq, j)),
        out_shape=jax.ShapeDtypeStruct((b, t, nh * LANES), _act_dtype(t)),
        compiler_params=_cparams("parallel", "parallel", "arbitrary"),
        name="sb_prompt",
    )(bias, qn3, kn3, proj3)


def _gather_pages(cache_k, cache_v, page_table):
    nl, n_pool, page, nh, dk = cache_k.shape
    b, npages = page_table.shape
    width = nh * dk
    ck = cache_k.reshape(nl, n_pool, page * nh, dk)
    cv = cache_v.reshape(nl, n_pool, page * nh, dk)

    pps = max(d for d in (4, 2, 1) if npages % d == 0)

    def kern(pt_ref, *refs):
        ins, (ok_ref, ov_ref) = refs[:2 * pps], refs[2 * pps:]
        for q in range(pps):
            rows = slice(q * page, (q + 1) * page)
            for hh in range(nh):
                lanes = slice(hh * dk, (hh + 1) * dk)
                ok_ref[rows, lanes] = ins[q][pl.ds(hh, page, stride=nh), :].astype(BF16)
                ov_ref[rows, lanes] = ins[pps + q][pl.ds(hh, page, stride=nh), :].astype(BF16)

    def in_spec(q):
        return pl.BlockSpec((None, None, page * nh, dk), lambda l, i, p, pt: (l, pt[i, pps * p + q], 0, 0))

    out_spec = pl.BlockSpec((None, None, pps * page, width), lambda l, i, p, pt: (l, i, p, 0))
    out_sds = jax.ShapeDtypeStruct((nl, b, npages * page, width), BF16)
    return pl.pallas_call(
        kern,
        grid_spec=pltpu.PrefetchScalarGridSpec(
            num_scalar_prefetch=1, grid=(nl, b, npages // pps),
            in_specs=[in_spec(q) for q in range(pps)] * 2, out_specs=[out_spec, out_spec]),
        out_shape=[out_sds, out_sds],
        compiler_params=_cparams("parallel", "parallel", "parallel"),
        name="gather_pages",
    )(page_table, *([ck] * pps + [cv] * pps))


def _sb_paged(qn3, kn3, proj3, v_blk0, bias, past_k, past_v, layer, *, nh):
    b, t, width = qn3.shape
    plen = past_k.shape[2]
    sk = LANES
    kb = _pick(plen, 512)
    nsub = kb // sk
    nblk = plen // kb
    ht = nh * t
    scale = LANES ** -0.5
    bias_rows = jnp.broadcast_to(jnp.repeat(bias, t)[:, None], (ht, sk)).astype(F32)
    assert v_blk0 % nh == 0 and sk >= t and plen % kb == 0 and kb % sk == 0

    def kern(q_ref, kn_ref, vn_ref, bias_ref, pk_ref, pv_ref, o_ref, acc_ref, r_ref, qb_ref):
        p = pl.program_id(1)
        u = (_iota2((sk, sk), 0) > _iota2((sk, sk), 1)).astype(BF16)

        def process(ks, v, mask):
            qb = qb_ref[...]
            z = [_dot_nt(qb, k) * scale + bias_ref[...] for k in ks]
            sp, rsum, suf = _sb_logs(z, [mask] * len(ks), u)
            r_run = r_ref[...]
            a = [None] * len(ks)
            for s in range(len(ks) - 1, -1, -1):
                a[s] = _bf(_sb_weight(z[s], sp[s], suf[s], r_run, mask))
                r_run = r_run + rsum[s]
            acc_ref[...] += _dot(a[0] if len(ks) == 1 else jnp.concatenate(a, axis=1), v)
            r_ref[...] = r_run

        @pl.when(p == 0)
        def _():
            blockmask = (_iota2((ht, width), 0) // t) == (_iota2((ht, width), 1) // LANES)
            qt = jnp.concatenate([q_ref[...]] * nh, axis=0)
            qb_ref[...] = jnp.where(blockmask, qt, 0.0).astype(BF16)
            acc_ref[...] = jnp.zeros_like(acc_ref)
            r_ref[...] = jnp.zeros_like(r_ref)
            pad = jnp.zeros((sk - t, width), F32)
            kpad = _bf(jnp.concatenate([kn_ref[...], pad], axis=0))
            vpad = _bf(jnp.concatenate([vn_ref[...], pad], axis=0))
            mask = _iota2((ht, sk), 1) < (_iota2((ht, sk), 0) % t)
            process([kpad], vpad, mask)

        process([pk_ref[s * sk:(s + 1) * sk, :] for s in range(nsub)], pv_ref[...], None)

        @pl.when(p == nblk - 1)
        def _():
            blockmask = (_iota2((ht, width), 0) // t) == (_iota2((ht, width), 1) // LANES)
            accm = jnp.where(blockmask, acc_ref[...], 0.0)
            out = accm[0:t, :]
            for hh in range(1, nh):
                out = out + accm[hh * t:(hh + 1) * t, :]
            o_ref[...] = out.astype(o_ref.dtype)

    past_spec = pl.BlockSpec((None, None, kb, width), lambda i, p: (layer, i, nblk - 1 - p, 0))
    return pl.pallas_call(
        kern,
        grid=(b, nblk),
        in_specs=[pl.BlockSpec((None, t, width), lambda i, p: (i, 0, 0)),
                  pl.BlockSpec((None, t, width), lambda i, p: (i, 0, 0)),
                  pl.BlockSpec((None, t, width), lambda i, p: (i, 0, v_blk0 // nh)),
                  pl.BlockSpec((ht, sk), lambda i, p: (0, 0)),
                  past_spec, past_spec],
        out_specs=pl.BlockSpec((None, t, width), lambda i, p: (i, 0, 0)),
        out_shape=jax.ShapeDtypeStruct((b, t, width), _act_dtype(t)),
        scratch_shapes=[pltpu.VMEM((ht, width), F32), pltpu.VMEM((ht, 1), F32), pltpu.VMEM((ht, width), BF16)],
        compiler_params=_cparams("parallel", "arbitrary"),
        name="sb_paged",
    )(qn3, kn3, proj3, bias_rows, past_k, past_v)


def _ffn_up(xn, w, layer, conv_w, buf, *, b, t):
    m, kdim = xn.shape
    f = w.shape[2] // 2
    kc = conv_w.shape[0]
    tm = _pick(t, 1024, 8)
    tn = _pick(f, 512)
    tpb = t // tm
    nfb = f // tn
    off = 8 - (kc - 1)
    sub = _pick(tm, 256, 8)
    assert m == b * t and t % tm == 0 and w.shape[1] == kdim

    def kern(x_ref, wa_ref, wb_ref, cw_ref, buf_ref, hid_ref, nb_ref, xs, wbf):
        i = pl.program_id(1)

        @pl.when(i == 0)
        def _():
            wbf[:, :tn] = _bf(wa_ref[...])
            wbf[:, tn:] = _bf(wb_ref[...])

        first = (i % tpb) == 0

        @pl.when(first)
        def _():
            xs[off:8, :] = buf_ref[...]

        @pl.when(jnp.logical_not(first))
        def _():
            xs[off:8, :] = xs[tm + off:tm + 8, :]

        def dots(s):
            ab = _dot(x_ref[s * sub:(s + 1) * sub, :], wbf[...])
            return ab[:, :tn], ab[:, tn:]

        def activate(s, a, gate):
            r0 = s * sub
            xs[8 + r0:8 + r0 + sub, :] = a
            y = xs[off + r0:off + r0 + sub, :] * cw_ref[0:1, :]
            for kk in range(1, kc):
                y = y + xs[off + kk + r0:off + kk + r0 + sub, :] * cw_ref[kk:kk + 1, :]
            hid_ref[r0:r0 + sub, :] = (_gelu_tanh(y) * gate).astype(hid_ref.dtype)

        nsub = tm // sub
        cur = dots(0)
        for s in range(1, nsub):
            nxt = dots(s)
            activate(s - 1, *cur)
            cur = nxt
        activate(nsub - 1, *cur)
        nb_ref[...] = xs[tm + off:tm + 8, :]

    return pl.pallas_call(
        kern,
        grid=(nfb, m // tm),
        in_specs=[pl.BlockSpec((tm, kdim), lambda j, i: (i, 0)),
                  pl.BlockSpec((None, kdim, tn), lambda j, i: (layer, 0, j)),
                  pl.BlockSpec((None, kdim, tn), lambda j, i: (layer, 0, nfb + j)),
                  pl.BlockSpec((kc, tn), lambda j, i: (0, j)),
                  pl.BlockSpec((None, kc - 1, tn), lambda j, i: (i // tpb, 0, j))],
        out_specs=[pl.BlockSpec((tm, tn), lambda j, i: (i, j)),
                   pl.BlockSpec((None, kc - 1, tn), lambda j, i: (i // tpb, 0, j))],
        out_shape=[jax.ShapeDtypeStruct((m, f), BF16), jax.ShapeDtypeStruct((b, kc - 1, f), F32)],
        scratch_shapes=[pltpu.VMEM((tm + 8, tn), F32), pltpu.VMEM((kdim, 2 * tn), BF16)],
        compiler_params=_cparams("arbitrary", "arbitrary"),
        name="ffn_up",
    )(xn, w, w, conv_w, buf)


def _ffn_act(u3, buf, conv_w):
    b, t, two_f = u3.shape
    f = two_f // 2
    kc = conv_w.shape[0]
    tc = _pick(f, 8192)
    nfb = f // tc
    off = 8 - (kc - 1)

    def kern(a_ref, b_ref, buf_ref, w_ref, hid_ref, nb_ref, xs):
        xs[off:8, :] = buf_ref[...]
        xs[8:8 + t, :] = a_ref[...]
        nb_ref[...] = xs[t + off:t + 8, :]
        y = xs[off:off + t, :] * w_ref[0:1, :]
        for kk in range(1, kc):
            y = y + xs[off + kk:off + kk + t, :] * w_ref[kk:kk + 1, :]
        hid_ref[...] = _gelu_tanh(y) * b_ref[...]

    return pl.pallas_call(
        kern,
        grid=(b, nfb),
        in_specs=[pl.BlockSpec((None, t, tc), lambda i, j: (i, 0, j)),
                  pl.BlockSpec((None, t, tc), lambda i, j: (i, 0, nfb + j)),
                  pl.BlockSpec((None, kc - 1, tc), lambda i, j: (i, 0, j)),
                  pl.BlockSpec((kc, tc), lambda i, j: (0, j))],
        out_specs=[pl.BlockSpec((None, t, tc), lambda i, j: (i, 0, j)),
                   pl.BlockSpec((None, kc - 1, tc), lambda i, j: (i, 0, j))],
        out_shape=[jax.ShapeDtypeStruct((b, t, f), F32), jax.ShapeDtypeStruct((b, kc - 1, f), F32)],
        scratch_shapes=[pltpu.VMEM((t + 8, tc), F32)],
        compiler_params=_cparams("parallel", "parallel"),
        name="ffn_act",
    )(u3, u3, buf, conv_w)


def _rot_tables(pos, dk):
    half = dk // 2
    inv = 1.0 / (RET_THETA ** (jnp.arange(half, dtype=F32) / half))
    ang = pos.astype(F32)[:, None] * inv[None, :]
    cos, sin = jnp.cos(ang), jnp.sin(ang)
    return jnp.concatenate([cos, cos], -1), jnp.concatenate([-sin, sin], -1)


def _trunk(x3, p4, pos, conv_a, rec_a, rec_b, rec_d, ffn_buf, W, dims, paged=None):
    b, t, d = x3.shape
    m = b * t
    (h_a, h_b, dv_b, h_c, h_d, dv_d, r_d) = dims
    qa = h_a * LANES
    conv_ch = 3 * qa
    qb = h_b * LANES
    vb = h_b * dv_b
    qc = h_c * LANES
    qd = h_d * LANES
    vd = h_d * dv_d
    depth = W["norm_mix"].shape[0]
    h2 = x3.reshape(m, d)
    cos_t, sin_t = _rot_tables(pos, LANES)
    log_gamma = jnp.log1p(-jnp.exp2(-5.0 - jnp.arange(h_b, dtype=F32)))
    ld_ret = jnp.broadcast_to(jnp.repeat(log_gamma, LANES)[None, None, :], (b, t, qb))
    conv_l, ra_l, rb_l, k_l, v_l, rd_l, ffn_l = [], [], [], [], [], [], []

    for i in range(depth):
        j = i // 2
        xn = _rms_cast(h2, W["norm_mix"][i])
        if i % 2 == 0:
            pre3 = _matmul(xn, W["w_in_even_pre"], j).reshape(b, t, conv_ch)
            proj3 = _matmul(xn, W["w_in_even_tail"], j).reshape(b, t, -1)
            sp3 = _matmul(xn, W["w_in_even_small"], j, tn_pref=LANES).reshape(b, t, LANES)
            qkv3, c_new = _conv_a_prep(pre3, conv_a[j], W["conv_a_w"][j], qa=qa, dk=LANES)
            o_a, sa_new = _gated_delta(qkv3, sp3, proj3, 0, W["alog_p"][j], W["dtb_p"][j],
                                       W["a_out_norm"][j].reshape(1, LANES), rec_a[j], nh=h_a)
            off_qb = qa
            o_b, sb_new = _gla(proj3, off_qb // LANES, (off_qb + qb) // LANES, (off_qb + 2 * qb) // dv_b,
                               (off_qb + 2 * qb + vb) // dv_b, W["b_gn"][j].reshape(h_b, 1, dv_b), rec_b[j],
                               nh=h_b, dv=dv_b, k_scale=LANES ** -0.5, norm="group",
                               rot=(cos_t, sin_t), ld3=ld_ret)
            h2 = _matmul_pair(o_a.astype(BF16).reshape(m, -1), o_b.astype(BF16).reshape(m, -1),
                              W["w_out_even"], j, h2)
            conv_l.append(c_new)
            ra_l.append(sa_new)
            rb_l.append(sb_new)
        else:
            proj3 = _matmul(xn, W["w_in_odd_main"], j).reshape(b, t, -1)
            sp3 = _matmul(xn, W["w_in_odd_small"], j, tn_pref=LANES).reshape(b, t, LANES)
            qn3 = _head_rms(proj3, 0, W["c_q_norm"][j], nh=h_c)
            kn3 = _head_rms(proj3, h_c, W["c_k_norm"][j], nh=h_c)
            if paged is None:
                o_c = _sb_prompt(qn3, kn3, proj3, 2 * h_c, W["c_logit_bias"][j], nh=h_c)
            else:
                past_k, past_v = paged
                o_c = _sb_paged(qn3, kn3, proj3, 2 * h_c, W["c_logit_bias"][j], past_k, past_v, j, nh=h_c)
            off_qd = 3 * qc
            o_d, sd_new = _gla(proj3, off_qd // LANES, (off_qd + qd) // LANES, (off_qd + 2 * qd) // dv_d,
                               (off_qd + 2 * qd + vd) // dv_d, W["d_gain"][j], rec_d[j],
                               nh=h_d, dv=dv_d, k_scale=LANES ** -0.5, norm="rms",
                               lowrank=(sp3, W["d_w2p"][j], W["d_alpha_b"][j].reshape(1, qd)))
            h2 = _matmul_pair(o_c.astype(BF16).reshape(m, -1), o_d.astype(BF16).reshape(m, -1),
                              W["w_out_odd"], j, h2)
            k_l.append(kn3.reshape(b, t, h_c, LANES))
            v_l.append(proj3[:, :, 2 * qc:3 * qc].reshape(b, t, h_c, LANES))
            rd_l.append(sd_new)
        xn = _rms_cast(h2, W["norm_ffn"][i])
        if t % BF16_SUBLANES == 0:
            hid, fb = _ffn_up(xn, W["w_ffn_in"], i, W["ffn_conv_w"][i], ffn_buf[i], b=b, t=t)
        else:
            u3 = _matmul(xn, W["w_ffn_in"], i).reshape(b, t, -1)
            hid, fb = _ffn_act(u3, ffn_buf[i], W["ffn_conv_w"][i])
            hid = hid.astype(BF16).reshape(m, -1)
        f = hid.shape[-1]
        tk_f = f // 2 if (f // 2) % LANES == 0 else f
        h2 = _matmul(hid, W["w_ffn_out"], i, res=h2, tk=tk_f)
        ffn_l.append(fb)
        xn = _rms_cast(h2, W["ple_norm"][i])
        h2 = _matmul(xn, W["w_ple_gate"], i, ple=(h2, p4[i].reshape(m, -1).astype(BF16), W["w_ple_in"]))
    return (h2.reshape(b, t, d), jnp.stack(conv_l), jnp.stack(ra_l), jnp.stack(rb_l), jnp.stack(k_l),
            jnp.stack(v_l), jnp.stack(rd_l), jnp.stack(ffn_l))


def kernel(x_prompt, x_sample, state_a_conv, state_a_rec, state_b_rec, cache_k, cache_v, state_d_rec, state_ffn_conv, page_table, p_prompt, p_sample, norm_mix, w_in_even, conv_a_w, a_log, dt_bias, a_out_norm, b_gn, w_out_even, w_in_odd, c_q_norm, c_k_norm, c_logit_bias, d_alpha_w2, d_alpha_b, d_out_norm, w_out_odd, norm_ffn, w_ffn_in, ffn_conv_w, w_ffn_out, ple_norm, w_ple_gate, w_ple_in):
    n_even, h_a = a_log.shape
    n_odd, h_c = c_logit_bias.shape
    h_b, dv_b = state_b_rec.shape[2], state_b_rec.shape[4]
    h_d, dv_d = state_d_rec.shape[2], state_d_rec.shape[4]
    r_d = d_alpha_w2.shape[1]
    depth = norm_mix.shape[0]
    conv_ch = conv_a_w.shape[2]
    qa = h_a * LANES
    qc = h_c * LANES
    qd = h_d * LANES
    assert conv_ch == 3 * qa and state_a_rec.shape[3:] == (LANES, LANES)
    assert state_b_rec.shape[3] == LANES and state_d_rec.shape[3] == LANES and cache_k.shape[4] == LANES
    assert 2 * h_a <= LANES and r_d <= LANES
    dims = (h_a, h_b, dv_b, h_c, h_d, dv_d, r_d)

    w_tail_e = w_in_even[:, :, conv_ch + 2 * h_a:].astype(BF16)
    w_small_e = jnp.pad(w_in_even[:, :, conv_ch:conv_ch + 2 * h_a],
                        ((0, 0), (0, 0), (0, LANES - 2 * h_a))).astype(BF16)
    n_main_o = 3 * qc + 2 * qd + 2 * h_d * dv_d
    w_small_o = jnp.pad(w_in_odd[:, :, n_main_o:], ((0, 0), (0, 0), (0, LANES - r_d))).astype(BF16)
    pad_ab = lambda v: jnp.pad(v, ((0, 0), (h_a, LANES - 2 * h_a)))[:, None, :]
    W = {
        "norm_mix": norm_mix, "norm_ffn": norm_ffn, "ple_norm": ple_norm,
        "w_in_even_pre": w_in_even[:, :, :conv_ch].astype(BF16), "w_in_even_tail": w_tail_e,
        "w_in_even_small": w_small_e,
        "w_in_odd_main": w_in_odd[:, :, :n_main_o].astype(BF16), "w_in_odd_small": w_small_o,
        "conv_a_w": conv_a_w, "alog_p": pad_ab(a_log), "dtb_p": pad_ab(dt_bias),
        "a_out_norm": a_out_norm, "b_gn": b_gn,
        "w_out_even": w_out_even, "w_out_odd": w_out_odd,
        "c_q_norm": c_q_norm, "c_k_norm": c_k_norm, "c_logit_bias": c_logit_bias,
        "d_w2p": jnp.pad(d_alpha_w2, ((0, 0), (0, LANES - r_d), (0, 0))), "d_alpha_b": d_alpha_b,
        "d_gain": jnp.broadcast_to(d_out_norm[:, None, None, :], (n_odd, h_d, 1, dv_d)),
        "w_ffn_in": w_ffn_in, "ffn_conv_w": ffn_conv_w, "w_ffn_out": w_ffn_out.astype(BF16),
        "w_ple_gate": w_ple_gate, "w_ple_in": w_ple_in.astype(BF16),
    }

    bp, tp = x_prompt.shape[0], x_prompt.shape[1]
    kconv = conv_a_w.shape[1]
    kffn = ffn_conv_w.shape[1]
    f = w_ffn_out.shape[1]
    zeros = lambda *s: jnp.zeros(s, F32)
    out_p = _trunk(x_prompt, p_prompt, jnp.arange(tp, dtype=jnp.int32),
                   zeros(n_even, bp, kconv - 1, conv_ch), zeros(n_even, bp, h_a, LANES, LANES),
                   zeros(n_even, bp, h_b, LANES, dv_b), zeros(n_odd, bp, h_d, LANES, dv_d),
                   zeros(depth, bp, kffn - 1, f), W, dims)

    ts = x_sample.shape[1]
    past_len = page_table.shape[1] * cache_k.shape[2]
    pos_s = past_len + jnp.arange(ts, dtype=jnp.int32)
    out_s = _trunk(x_sample, p_sample, pos_s, state_a_conv, state_a_rec, state_b_rec, state_d_rec,
                   state_ffn_conv, W, dims, paged=_gather_pages(cache_k, cache_v, page_table))

    res = []
    for a, s in zip(out_p, out_s):
        res += [a, s]
    return tuple(res)
```

```python
import math

import jax
import jax.numpy as jnp
from jax import lax
from jax.experimental import pallas as pl
from jax.experimental.pallas import tpu as pltpu

F32 = jnp.float32
BF16 = jnp.bfloat16
EPS = 1e-6
LANES = 128
BF16_SUBLANES = 16
CHUNK = 64
SUB = 16
RET_THETA = 10000.0
TAU_D = 16.0
VMEM_LIMIT = 56 * 1024 * 1024


def _cparams(*sem):
    return pltpu.CompilerParams(dimension_semantics=sem, vmem_limit_bytes=VMEM_LIMIT)


def _pick(n, pref, mult=LANES):
    t = (min(pref, n) // mult) * mult
    while t >= mult:
        if n % t == 0:
            return t
        t -= mult
    return n


def _act_dtype(t):
    return BF16 if t % BF16_SUBLANES == 0 else F32


def _dot(a, b):
    return jnp.dot(a, b, preferred_element_type=F32)


def _dot_nt(a, b):
    return lax.dot_general(a, b, (((1,), (1,)), ((), ())), preferred_element_type=F32)


def _dot_tn(a, b):
    return lax.dot_general(a, b, (((0,), (0,)), ((), ())), preferred_element_type=F32)


def _bf(x):
    return x.astype(BF16)


def _silu(x):
    return x * jax.nn.sigmoid(x)


def _softplus(x):
    return jnp.maximum(x, 0.0) + jnp.log1p(jnp.exp(-jnp.abs(x)))


def _log_sigmoid(x):
    return jnp.minimum(x, 0.0) - jnp.log1p(jnp.exp(-jnp.abs(x)))


def _gelu_tanh(x):
    c = math.sqrt(2.0 / math.pi)
    return 0.5 * x * (1.0 + jnp.tanh(c * (x + 0.044715 * (x * x * x))))


def _iota2(shape, dim):
    return lax.broadcasted_iota(jnp.int32, shape, dim)


def _split3(x):
    hi = x.astype(BF16)
    r1 = x - hi.astype(F32)
    mid = r1.astype(BF16)
    lo = (r1 - mid.astype(F32)).astype(BF16)
    return hi, mid, lo


def _dot01(m01, x):
    hi, mid, lo = _split3(x)
    return _dot(m01, hi) + _dot(m01, mid) + _dot(m01, lo)


def _rms_cast(x2, g):
    m, d = x2.shape
    tm = _pick(m, 512, 8)

    def kern(x_ref, g_ref, o_ref):
        x = x_ref[...]
        r = lax.rsqrt(jnp.mean(x * x, axis=-1, keepdims=True) + EPS)
        o_ref[...] = (x * r * g_ref[...]).astype(o_ref.dtype)

    return pl.pallas_call(
        kern,
        grid=(m // tm,),
        in_specs=[pl.BlockSpec((tm, d), lambda i: (i, 0)), pl.BlockSpec((1, d), lambda i: (0, 0))],
        out_specs=pl.BlockSpec((tm, d), lambda i: (i, 0)),
        out_shape=jax.ShapeDtypeStruct((m, d), BF16),
        compiler_params=_cparams("parallel"),
        name="rms_cast",
    )(x2, g.reshape(1, d))


def _matmul(x, w, layer, *, res=None, ple=None, tm_pref=1024, tn_pref=512, tk=None):
    m, kdim = x.shape
    n = w.shape[2]
    tm = _pick(m, tm_pref, 8)
    tn = _pick(n, tn_pref)
    tk = kdim if tk is None else tk
    nk = kdim // tk
    assert kdim % tk == 0 and m % tm == 0 and n % tn == 0 and w.shape[1] == kdim

    in_specs = [pl.BlockSpec((tm, tk), lambda i, j, k: (i, k)),
                pl.BlockSpec((None, tk, tn), lambda i, j, k: (layer, k, j))]
    args = [x, w]
    if res is not None:
        in_specs.append(pl.BlockSpec((tm, tn), lambda i, j, k: (i, j)))
        args.append(res)
    if ple is not None:
        h, p, w_in = ple
        pdim = p.shape[1]
        in_specs += [pl.BlockSpec((tm, tn), lambda i, j, k: (i, j)),
                     pl.BlockSpec((tm, pdim), lambda i, j, k: (i, 0)),
                     pl.BlockSpec((None, pdim, tn), lambda i, j, k: (layer, 0, j))]
        args += [h, p, w_in]

    def kern(*refs):
        x_ref, w_ref = refs[0], refs[1]
        o_ref = refs[len(args)]
        acc_ref = refs[len(args) + 1] if nk > 1 else None

        def epilogue(acc):
            if res is not None:
                return refs[2][...] + acc
            if ple is not None:
                h_ref, p_ref, wi_ref = refs[2], refs[3], refs[4]
                emb = _dot(p_ref[...], wi_ref[...])
                return h_ref[...] + jax.nn.sigmoid(acc) * emb
            return acc

        part = _dot(x_ref[...], _bf(w_ref[...]))
        if nk == 1:
            o_ref[...] = epilogue(part)
        else:
            k = pl.program_id(2)

            @pl.when(k == 0)
            def _():
                acc_ref[...] = part

            @pl.when(jnp.logical_and(k > 0, k < nk - 1))
            def _():
                acc_ref[...] += part

            @pl.when(k == nk - 1)
            def _():
                o_ref[...] = epilogue(acc_ref[...] + part)

    return pl.pallas_call(
        kern,
        grid=(m // tm, n // tn, nk),
        in_specs=in_specs,
        out_specs=pl.BlockSpec((tm, tn), lambda i, j, k: (i, j)),
        out_shape=jax.ShapeDtypeStruct((m, n), F32),
        scratch_shapes=[pltpu.VMEM((tm, tn), F32)] if nk > 1 else [],
        compiler_params=_cparams("parallel", "parallel", "arbitrary"),
        name="matmul",
    )(*args)


def _matmul_pair(xa, xb, w, layer, res):
    m, kh = xa.shape
    n = w.shape[2]
    tm = _pick(m, 1024, 8)
    tn = _pick(n, 512)
    assert xb.shape == (m, kh) and w.shape[1] == 2 * kh

    def kern(xa_ref, xb_ref, wa_ref, wb_ref, res_ref, o_ref):
        o_ref[...] = res_ref[...] + (_dot(xa_ref[...], _bf(wa_ref[...])) + _dot(xb_ref[...], _bf(wb_ref[...])))

    return pl.pallas_call(
        kern,
        grid=(m // tm, n // tn),
        in_specs=[pl.BlockSpec((tm, kh), lambda i, j: (i, 0)), pl.BlockSpec((tm, kh), lambda i, j: (i, 0)),
                  pl.BlockSpec((None, kh, tn), lambda i, j: (layer, 0, j)),
                  pl.BlockSpec((None, kh, tn), lambda i, j: (layer, 1, j)),
                  pl.BlockSpec((tm, tn), lambda i, j: (i, j))],
        out_specs=pl.BlockSpec((tm, tn), lambda i, j: (i, j)),
        out_shape=jax.ShapeDtypeStruct((m, n), F32),
        compiler_params=_cparams("parallel", "parallel"),
        name="matmul_pair",
    )(xa, xb, w, w, res)


def _load_chunk(ref, r0, t):
    if t >= CHUNK:
        return ref[pl.ds(r0, CHUNK), :]
    x = ref[...]
    return jnp.concatenate([x, jnp.zeros((CHUNK - t, x.shape[1]), x.dtype)], axis=0)


def _store_chunk(ref, r0, t, lanes, val):
    if t >= CHUNK:
        ref[pl.ds(r0, CHUNK), lanes] = val.astype(ref.dtype)
    else:
        ref[:, lanes] = val[:t].astype(ref.dtype)


def _time_tile(t):
    tt = min(t, 512)
    assert t % tt == 0 and (tt % CHUNK == 0 or t < CHUNK)
    return tt


def _inv_unit_lower(a, eye, same_sub, nb):
    a_d = [jnp.where(same_sub, x, 0.0) for x in a]
    low = [x - y for x, y in zip(a, a_d)]
    dinv = [eye - x for x in a_d]
    pb = [_bf(-x) for x in a_d]
    s = 2
    while s < SUB:
        pb = [_bf(_dot(x, x)) for x in pb]
        dinv = [d + _dot(_bf(d), x) for d, x in zip(dinv, pb)]
        s *= 2
    if nb == 1:
        return dinv
    db = [_bf(d) for d in dinv]
    nmat = [_dot(d, _bf(x)) for d, x in zip(db, low)]
    nb16 = [_bf(x) for x in nmat]
    acc = [eye - x for x in nmat]
    pw = nb16
    sign = -1.0
    for _ in range(2, nb):
        pwm = [_dot(x, y) for x, y in zip(pw, nb16)]
        pw = [_bf(x) for x in pwm]
        sign = -sign
        acc = [x + sign * y for x, y in zip(acc, pwm)]
    return [_dot(_bf(x), d) for x, d in zip(acc, db)]


def _conv_a_prep(proj3, buf, conv_w, *, qa, dk):
    b, t, _ = proj3.shape
    kc, ch = conv_w.shape
    tc = _pick(qa, 512)
    nqb = qa // tc
    assert ch % tc == 0
    off = 8 - (kc - 1)

    def kern(x_ref, buf_ref, w_ref, o_ref, cn_ref, xs):
        j = pl.program_id(1)
        xs[off:8, :] = buf_ref[...]
        xs[8:8 + t, :] = x_ref[...]
        cn_ref[...] = xs[t + off:t + 8, :]
        is_q = j < nqb
        is_k = j < 2 * nqb
        scale = jnp.where(is_q, dk ** -0.5, 1.0).astype(F32)
        for hh in range(tc // LANES):
            sl = slice(hh * LANES, (hh + 1) * LANES)
            y = xs[off:off + t, sl] * w_ref[0:1, sl]
            for kk in range(1, kc):
                y = y + xs[off + kk:off + kk + t, sl] * w_ref[kk:kk + 1, sl]
            s = _silu(y)
            nrm = s * lax.rsqrt(jnp.sum(s * s, axis=-1, keepdims=True) + EPS) * scale
            o_ref[:, sl] = jnp.where(is_k, nrm, s)

    return pl.pallas_call(
        kern,
        grid=(b, ch // tc),
        in_specs=[pl.BlockSpec((None, t, tc), lambda i, j: (i, 0, j)),
                  pl.BlockSpec((None, kc - 1, tc), lambda i, j: (i, 0, j)),
                  pl.BlockSpec((kc, tc), lambda i, j: (0, j))],
        out_specs=[pl.BlockSpec((None, t, tc), lambda i, j: (i, 0, j)),
                   pl.BlockSpec((None, kc - 1, tc), lambda i, j: (i, 0, j))],
        out_shape=[jax.ShapeDtypeStruct((b, t, ch), F32), jax.ShapeDtypeStruct((b, kc - 1, ch), F32)],
        scratch_shapes=[pltpu.VMEM((t + 8, tc), F32)],
        compiler_params=_cparams("parallel", "parallel"),
        name="conv_a_prep",
    )(proj3, buf, conv_w)


def _gated_delta(qkv3, sp3, proj3, gate_blk0, alog_p, dtb_p, a_norm, s0, *, nh):
    b, t, _ = qkv3.shape
    c = CHUNK
    hs = min(16, nh)
    gh = min(2, hs)
    gs = hs // gh
    n = gh * c
    tt = _time_tile(t)
    nchunks = max(tt // c, 1)
    wl = hs * LANES
    assert nh % hs == 0 and hs % gh == 0 and gate_blk0 % hs == 0
    odt = _act_dtype(t)

    def kern(q_ref, k_ref, v_ref, sp_ref, gate_ref, alog_ref, dtb_ref, an_ref, s0_ref, o_ref, s_ref):
        hb = pl.program_id(1)
        ti = pl.program_id(2)

        @pl.when(ti == 0)
        def _():
            s_ref[...] = s0_ref[...]

        row = _iota2((n, n), 0)
        col = _iota2((n, n), 1)
        same = (row // c) == (col // c)
        incl = jnp.logical_and(same, col <= row)
        strict = jnp.logical_and(same, col < row)
        same_sub = (row // SUB) == (col // SUB)
        lbd = incl.astype(BF16)
        eye = (row == col).astype(F32)
        lane = _iota2((n, LANES), 1)
        rown = _iota2((n, 1), 0)
        valid = (rown % c) < t
        neg_a = -jnp.exp(alog_ref[...])
        dtb = dtb_ref[...]
        an = an_ref[...]

        def stack(x, g0):
            return jnp.concatenate([x[:, (g0 + i) * LANES:(g0 + i + 1) * LANES] for i in range(gh)], axis=0)

        def chunk(ci, carry):
            r0 = pl.multiple_of(ci * c, c)
            qc = _load_chunk(q_ref, r0, tt)
            kc = _load_chunk(k_ref, r0, tt)
            vc = _load_chunk(v_ref, r0, tt)
            gc = _load_chunk(gate_ref, r0, tt)
            sp = _load_chunk(sp_ref, r0, tt)
            beta_all = jax.nn.sigmoid(sp)
            g_all = neg_a * _softplus(sp + dtb)
            bt = jnp.concatenate([beta_all] * gh, axis=0)
            gt = jnp.concatenate([g_all] * gh, axis=0)
            grp = range(gs)
            rowhead = [hb * hs + gi * gh + rown // c for gi in grp]
            q = [stack(qc, gi * gh) for gi in grp]
            k = [stack(kc, gi * gh) for gi in grp]
            v = [stack(vc, gi * gh) for gi in grp]
            s_old = [[s_ref[gi * gh + i] for i in range(gh)] for gi in grp]
            bcol = [jnp.sum(jnp.where(jnp.logical_and(lane == rh, valid), bt, 0.0), axis=1, keepdims=True)
                    for rh in rowhead]
            g_m = [jnp.where(jnp.logical_and(lane == nh + rh, valid), gt, 0.0) for rh in rowhead]
            gcol = [jnp.sum(_dot01(lbd, x), axis=1, keepdims=True) for x in g_m]
            grow = [jnp.sum(eye * x, axis=0, keepdims=True) for x in gcol]
            decay = [jnp.where(incl, jnp.exp(jnp.where(incl, x - y, 0.0)), 0.0) for x, y in zip(gcol, grow)]
            eg = [jnp.exp(x) for x in gcol]
            kb = [_bf(x) for x in k]
            kk = [_dot_nt(x, x) for x in kb]
            amat = [jnp.where(strict, x * d * bc, 0.0) for x, d, bc in zip(kk, decay, bcol)]
            inv = _inv_unit_lower(amat, eye, same_sub, c // SUB)
            rhs = [jnp.concatenate([vv * bc, kx * (bc * e)], axis=1)
                   for vv, kx, bc, e in zip(v, k, bcol, eg)]
            sol = [_dot(_bf(x), _bf(y)) for x, y in zip(inv, rhs)]
            qe = [x * e for x, e in zip(q, eg)]
            sob = [[_bf(x) for x in row_] for row_ in s_old]
            u = [jnp.concatenate([sol[gi][i * c:(i + 1) * c, :LANES]
                                  - _dot(_bf(sol[gi][i * c:(i + 1) * c, LANES:]), sob[gi][i])
                                  for i in range(gh)], axis=0) for gi in grp]
            oi = [jnp.concatenate([_dot(_bf(qe[gi][i * c:(i + 1) * c]), sob[gi][i]) for i in range(gh)], axis=0)
                  for gi in grp]
            ub = [_bf(x) for x in u]
            attn = [_dot_nt(_bf(x), y) * d for x, y, d in zip(q, kb, decay)]
            o = [x + _dot(_bf(y), z) for x, y, z in zip(oi, attn, ub)]
            on = [x * lax.rsqrt(jnp.mean(x * x, axis=-1, keepdims=True) + EPS) * an for x in o]
            for gi in grp:
                for i in range(gh):
                    rs = slice(i * c, (i + 1) * c)
                    lanes = slice((gi * gh + i) * LANES, (gi * gh + i + 1) * LANES)
                    g_last = gcol[gi][(i + 1) * c - 1:(i + 1) * c, :]
                    k_dec = k[gi][rs] * jnp.exp(g_last - gcol[gi][rs])
                    s_ref[gi * gh + i] = s_old[gi][i] * jnp.exp(g_last) + _dot_tn(_bf(k_dec), ub[gi][rs])
                    _store_chunk(o_ref, r0, tt, lanes, on[gi][rs] * _silu(gc[:, lanes]))
            return carry

        if nchunks == 1:
            chunk(0, 0)
        else:
            lax.fori_loop(0, nchunks, chunk, 0)

    blk = lambda off: pl.BlockSpec((None, tt, wl), lambda i, j, ti: (i, ti, off + j))
    vec = pl.BlockSpec((1, LANES), lambda i, j, ti: (0, 0))
    st = pl.BlockSpec((None, hs, LANES, LANES), lambda i, j, ti: (i, j, 0, 0))
    return pl.pallas_call(
        kern,
        grid=(b, nh // hs, t // tt),
        in_specs=[blk(0), blk(nh // hs), blk(2 * nh // hs),
                  pl.BlockSpec((None, tt, LANES), lambda i, j, ti: (i, ti, 0)),
                  blk(gate_blk0 // hs), vec, vec, vec, st],
        out_specs=[blk(0), st],
        out_shape=[jax.ShapeDtypeStruct((b, t, nh * LANES), odt),
                   jax.ShapeDtypeStruct((b, nh, LANES, LANES), F32)],
        compiler_params=_cparams("parallel", "parallel", "arbitrary"),
        name="gated_delta",
    )(qkv3, qkv3, qkv3, sp3, proj3, alog_p, dtb_p, a_norm, s0)


def _gla(proj3, q_blk0, k_blk0, v_blk0, gate_blk0, gain, s0, *, nh, dv, k_scale, norm,
         rot=None, ld3=None, lowrank=None):
    b, t, _ = proj3.shape
    c = CHUNK
    sb = SUB
    nsb = c // sb
    dk = LANES
    hg = min(8, nh)
    tt = _time_tile(t)
    nchunks = max(tt // c, 1)
    assert nh % hg == 0 and q_blk0 % hg == 0 and k_blk0 % hg == 0 and v_blk0 % hg == 0 and gate_blk0 % hg == 0
    odt = _act_dtype(t)
    n_in = 5 + (2 if rot is not None else 0) + (1 if ld3 is not None else 0) + (3 if lowrank is not None else 0)

    def kern(*refs):
        q_ref, k_ref, v_ref, gate_ref, gain_ref = refs[:5]
        pos = 5
        if rot is not None:
            cos_ref, sin_ref = refs[pos], refs[pos + 1]
            pos += 2
        if ld3 is not None:
            ld_ref = refs[pos]
            pos += 1
        if lowrank is not None:
            sp_ref, w2_ref, ab_ref = refs[pos:pos + 3]
            pos += 3
        s0_ref = refs[pos]
        o_ref, s_ref = refs[n_in + 1], refs[n_in + 2]
        st_ref = refs[n_in + 3]
        ti = pl.program_id(2)

        @pl.when(ti == 0)
        def _():
            for hh in range(hg):
                st_ref[hh] = s0_ref[hh].T

        row = _iota2((c, c), 0)
        col = _iota2((c, c), 1)
        lincl = (col <= row).astype(BF16)
        same = (row // sb) == (col // sb)
        diag_mask = jnp.logical_and(same, col <= row)
        below = (col // sb) < (row // sb)
        colmasks = [jnp.logical_and(same, (col % sb) == j) for j in range(sb)]
        rvalid = _iota2((c, 1), 0) < t

        def heads(q, k, v, ld, gate):
            hd = range(hg)
            g = [_dot01(lincl, x) for x in ld]
            st = [st_ref[hh] for hh in hd]
            o = [_dot_nt(_bf(x * jnp.exp(y)), _bf(s)) for x, y, s in zip(q, g, st)]
            offs = [[jnp.zeros((sb, c), F32)] for _ in hd]
            for a in range(1, nsb):
                gs = [x[a * sb - 1:a * sb, :] for x in g]
                qt = [x[a * sb:(a + 1) * sb, :] * jnp.exp(y[a * sb:(a + 1) * sb, :] - z)
                      for x, y, z in zip(q, g, gs)]
                kt = [x * jnp.exp(jnp.minimum(z - y, 0.0)) for x, y, z in zip(k, g, gs)]
                for hh in hd:
                    offs[hh].append(_dot_nt(_bf(qt[hh]), _bf(kt[hh])))
            s_off = [jnp.concatenate(x, axis=0) for x in offs]
            g3 = [x.reshape(nsb, sb, dk) for x in g]
            q3 = [x.reshape(nsb, sb, dk) for x in q]
            k3 = [x.reshape(nsb, sb, dk) for x in k]
            pd = [jnp.zeros((c, c), F32) for _ in hd]
            for j in range(sb):
                e = [jnp.exp(jnp.minimum(x - x[:, j:j + 1, :], 0.0)) for x in g3]
                colj = [jnp.sum(x * y[:, j:j + 1, :] * z, axis=-1, keepdims=True).reshape(c, 1)
                        for x, y, z in zip(q3, k3, e)]
                pd = [jnp.where(colmasks[j], x, y) for x, y in zip(colj, pd)]
            scores = [jnp.where(diag_mask, x, jnp.where(below, y, 0.0)) for x, y in zip(pd, s_off)]
            vb = [_bf(x) for x in v]
            o = [x + _dot(_bf(y), z) for x, y, z in zip(o, scores, vb)]
            g_last = [x[c - 1:c, :] for x in g]
            k_dec = [x * jnp.exp(y - z) for x, y, z in zip(k, g_last, g)]
            for hh in hd:
                st_ref[hh] = st[hh] * jnp.exp(g_last[hh]) + _dot_tn(vb[hh], _bf(k_dec[hh]))
            if norm == "group":
                oc = [x - jnp.mean(x, axis=-1, keepdims=True) for x in o]
                on = [x * lax.rsqrt(jnp.mean(x * x, axis=-1, keepdims=True) + EPS) for x in oc]
            else:
                on = [x * lax.rsqrt(jnp.mean(x * x, axis=-1, keepdims=True) + EPS) for x in o]
            return [on[hh] * gain_ref[hh] * _silu(gate[hh]) for hh in hd]

        def chunk(ci, carry):
            r0 = pl.multiple_of(ci * c, c)
            qc = _load_chunk(q_ref, r0, tt)
            kc = _load_chunk(k_ref, r0, tt)
            vc = _load_chunk(v_ref, r0, tt)
            gatec = _load_chunk(gate_ref, r0, tt)
            if rot is not None:
                cs = _load_chunk(cos_ref, r0, tt)
                sn = _load_chunk(sin_ref, r0, tt)
            if ld3 is not None:
                ldc = _load_chunk(ld_ref, r0, tt)
            else:
                sp = _load_chunk(sp_ref, r0, tt)
                logit = _dot(_bf(sp), _bf(w2_ref[...])) + ab_ref[...]
                ldc = jnp.where(rvalid, _log_sigmoid(logit) / TAU_D, 0.0)
            ks = [slice(hh * dk, (hh + 1) * dk) for hh in range(hg)]
            vs = [slice(hh * dv, (hh + 1) * dv) for hh in range(hg)]
            q = [qc[:, s] for s in ks]
            k = [kc[:, s] for s in ks]
            if rot is not None:
                q = [x * cs + pltpu.roll(x, dk // 2, axis=1) * sn for x in q]
                k = [x * cs + pltpu.roll(x, dk // 2, axis=1) * sn for x in k]
            k = [x * k_scale for x in k]
            out = heads(q, k, [vc[:, s] for s in vs], [ldc[:, s] for s in ks], [gatec[:, s] for s in vs])
            for hh in range(hg):
                _store_chunk(o_ref, r0, tt, vs[hh], out[hh])
            return carry

        if nchunks == 1:
            chunk(0, 0)
        else:
            lax.fori_loop(0, nchunks, chunk, 0)

        @pl.when(ti == pl.num_programs(2) - 1)
        def _():
            for hh in range(hg):
                s_ref[hh] = st_ref[hh].T

    def blk(off, width):
        return pl.BlockSpec((None, tt, hg * width), lambda i, j, ti: (i, ti, off // hg + j))

    in_specs = [blk(q_blk0, dk), blk(k_blk0, dk), blk(v_blk0, dv), blk(gate_blk0, dv),
                pl.BlockSpec((hg, 1, dv), lambda i, j, ti: (j, 0, 0))]
    args = [proj3, proj3, proj3, proj3, gain]
    if rot is not None:
        in_specs += [pl.BlockSpec((tt, dk), lambda i, j, ti: (ti, 0))] * 2
        args += list(rot)
    if ld3 is not None:
        in_specs.append(blk(0, dk))
        args.append(ld3)
    if lowrank is not None:
        sp3, w2p, ab = lowrank
        in_specs += [pl.BlockSpec((None, tt, LANES), lambda i, j, ti: (i, ti, 0)),
                     pl.BlockSpec((LANES, hg * dk), lambda i, j, ti: (0, j)),
                     pl.BlockSpec((1, hg * dk), lambda i, j, ti: (0, j))]
        args += [sp3, w2p, ab]
    st_spec = pl.BlockSpec((None, hg, dk, dv), lambda i, j, ti: (i, j, 0, 0))
    in_specs.append(st_spec)
    args.append(s0)
    assert len(args) == n_in + 1

    return pl.pallas_call(
        kern,
        grid=(b, nh // hg, t // tt),
        in_specs=in_specs,
        out_specs=[blk(0, dv), st_spec],
        out_shape=[jax.ShapeDtypeStruct((b, t, nh * dv), odt),
                   jax.ShapeDtypeStruct((b, nh, dk, dv), F32)],
        scratch_shapes=[pltpu.VMEM((hg, dv, dk), F32)],
        compiler_params=_cparams("parallel", "parallel", "arbitrary"),
        name="gla_" + norm,
    )(*args)


def _head_rms(proj3, blk0, gain, *, nh):
    b, t, _ = proj3.shape
    width = nh * LANES
    tc = _pick(width, 512)
    per = tc // LANES
    c0 = blk0 // per
    assert blk0 % per == 0

    def kern(x_ref, g_ref, o_ref):
        g = g_ref[...]
        for hh in range(per):
            sl = slice(hh * LANES, (hh + 1) * LANES)
            x = x_ref[:, sl]
            o_ref[:, sl] = x * lax.rsqrt(jnp.mean(x * x, axis=-1, keepdims=True) + EPS) * g

    return pl.pallas_call(
        kern,
        grid=(b, width // tc),
        in_specs=[pl.BlockSpec((None, t, tc), lambda i, j: (i, 0, c0 + j)),
                  pl.BlockSpec((1, LANES), lambda i, j: (0, 0))],
        out_specs=pl.BlockSpec((None, t, tc), lambda i, j: (i, 0, j)),
        out_shape=jax.ShapeDtypeStruct((b, t, width), F32),
        compiler_params=_cparams("parallel", "parallel"),
        name="head_rms",
    )(proj3, gain.reshape(1, LANES))


def _sb_logs(z, masks, u):
    sp = [jnp.maximum(x, 0.0) + jnp.log(1.0 + jnp.exp(-jnp.abs(x))) for x in z]
    l1 = [-x if m is None else jnp.where(m, -x, 0.0) for x, m in zip(sp, masks)]
    rsum = [jnp.sum(x, axis=1, keepdims=True) for x in l1]
    suf = [_dot(_bf(x), u) for x in l1]
    return sp, rsum, suf


def _sb_weight(z, sp, suf, r_run, mask):
    w = jnp.exp((z - sp) + (suf + r_run))
    return w if mask is None else jnp.where(mask, w, 0.0)


def _sb_prompt(qn3, kn3, proj3, v_blk0, bias, *, nh):
    b, t, _ = qn3.shape
    bq = min(256, t)
    bk = min(128, t)
    ratio = bq // bk
    hp = min(4, nh)
    assert t % bq == 0 and bq % bk == 0 and nh % hp == 0 and v_blk0 % hp == 0
    scale = LANES ** -0.5

    def kern(bias_ref, q_ref, k_ref, v_ref, o_ref):
        hb = pl.program_id(1)
        qi = pl.program_id(2)
        u = (_iota2((bk, bk), 0) > _iota2((bk, bk), 1)).astype(BF16)
        dmask = [_iota2((bq, bk), 1) + s * bk < _iota2((bq, bk), 0) for s in range(ratio)]
        qs = [_bf(q_ref[:, hh * LANES:(hh + 1) * LANES]) for hh in range(hp)]
        biases = [bias_ref[hb * hp + hh] for hh in range(hp)]

        def step(kblk, carry, masks):
            pairs = [(hh, s) for hh in range(hp) for s in range(ratio - 1, -1, -1)]
            k0 = {s: pl.multiple_of(kblk * bq + s * bk, bk) for s in range(ratio)}
            lanes = [slice(hh * LANES, (hh + 1) * LANES) for hh in range(hp)]
            kk = [_bf(k_ref[pl.ds(k0[s], bk), lanes[hh]]) for hh, s in pairs]
            vv = [_bf(v_ref[pl.ds(k0[s], bk), lanes[hh]]) for hh, s in pairs]
            z = [_dot_nt(qs[hh], x) * scale + biases[hh] for (hh, s), x in zip(pairs, kk)]
            pmask = [None if masks is None else masks[s] for hh, s in pairs]
            sp, rsum, suf = _sb_logs(z, pmask, u)
            new = []
            for hh in range(hp):
                r_run, acc = carry[hh]
                for idx, (h2, s) in enumerate(pairs):
                    if h2 != hh:
                        continue
                    a = _sb_weight(z[idx], sp[idx], suf[idx], r_run, pmask[idx])
                    acc = acc + _dot(_bf(a), vv[idx])
                    r_run = r_run + rsum[idx]
                new.append((r_run, acc))
            return tuple(new)

        init = tuple((jnp.zeros((bq, 1), F32), jnp.zeros((bq, LANES), F32)) for _ in range(hp))
        carry = step(qi, init, dmask)
        carry = lax.fori_loop(0, qi, lambda i, cr: step(qi - 1 - i, cr, None), carry)
        for hh in range(hp):
            o_ref[:, hh * LANES:(hh + 1) * LANES] = carry[hh][1].astype(o_ref.dtype)

    wl = hp * LANES
    return pl.pallas_call(
        kern,
        grid=(b, nh // hp, t // bq),
        in_specs=[pl.BlockSpec(memory_space=pltpu.SMEM),
                  pl.BlockSpec((None, bq, wl), lambda i, j, qq: (i, qq, j)),
                  pl.BlockSpec((None, t, wl), lambda i, j, qq: (i, 0, j)),
                  pl.BlockSpec((None, t, wl), lambda i, j, qq: (i, 0, v_blk0 // hp + j))],
        out_specs=pl.BlockSpec((None, bq, wl), lambda i, j, qq: (i, qq, j)),
        out_shape=jax.ShapeDtypeStruct((b, t, nh * LANES), _act_dtype(t)),
        compiler_params=_cparams("parallel", "parallel", "arbitrary"),
        name="sb_prompt",
    )(bias, qn3, kn3, proj3)


def _gather_pages(cache_k, cache_v, page_table):
    nl, n_pool, page, nh, dk = cache_k.shape
    b, npages = page_table.shape
    width = nh * dk
    ck = cache_k.reshape(nl, n_pool, page * nh, dk)
    cv = cache_v.reshape(nl, n_pool, page * nh, dk)

    pps = max(d for d in (4, 2, 1) if npages % d == 0)

    def kern(pt_ref, *refs):
        ins, (ok_ref, ov_ref) = refs[:2 * pps], refs[2 * pps:]
        for q in range(pps):
            rows = slice(q * page, (q + 1) * page)
            for hh in range(nh):
                lanes = slice(hh * dk, (hh + 1) * dk)
                ok_ref[rows, lanes] = ins[q][pl.ds(hh, page, stride=nh), :].astype(BF16)
                ov_ref[rows, lanes] = ins[pps + q][pl.ds(hh, page, stride=nh), :].astype(BF16)

    def in_spec(q):
        return pl.BlockSpec((None, None, page * nh, dk), lambda l, i, p, pt: (l, pt[i, pps * p + q], 0, 0))

    out_spec = pl.BlockSpec((None, None, pps * page, width), lambda l, i, p, pt: (l, i, p, 0))
    out_sds = jax.ShapeDtypeStruct((nl, b, npages * page, width), BF16)
    return pl.pallas_call(
        kern,
        grid_spec=pltpu.PrefetchScalarGridSpec(
            num_scalar_prefetch=1, grid=(nl, b, npages // pps),
            in_specs=[in_spec(q) for q in range(pps)] * 2, out_specs=[out_spec, out_spec]),
        out_shape=[out_sds, out_sds],
        compiler_params=_cparams("parallel", "parallel", "parallel"),
        name="gather_pages",
    )(page_table, *([ck] * pps + [cv] * pps))


def _sb_paged(qn3, kn3, proj3, v_blk0, bias, past_k, past_v, layer, *, nh):
    b, t, width = qn3.shape
    plen = past_k.shape[2]
    sk = LANES
    kb = _pick(plen, 512)
    nsub = kb // sk
    nblk = plen // kb
    ht = nh * t
    scale = LANES ** -0.5
    bias_rows = jnp.broadcast_to(jnp.repeat(bias, t)[:, None], (ht, sk)).astype(F32)
    assert v_blk0 % nh == 0 and sk >= t and plen % kb == 0 and kb % sk == 0

    def kern(q_ref, kn_ref, vn_ref, bias_ref, pk_ref, pv_ref, o_ref, acc_ref, r_ref, qb_ref):
        p = pl.program_id(1)
        u = (_iota2((sk, sk), 0) > _iota2((sk, sk), 1)).astype(BF16)

        def process(ks, v, mask):
            qb = qb_ref[...]
            z = [_dot_nt(qb, k) * scale + bias_ref[...] for k in ks]
            sp, rsum, suf = _sb_logs(z, [mask] * len(ks), u)
            r_run = r_ref[...]
            a = [None] * len(ks)
            for s in range(len(ks) - 1, -1, -1):
                a[s] = _bf(_sb_weight(z[s], sp[s], suf[s], r_run, mask))
                r_run = r_run + rsum[s]
            acc_ref[...] += _dot(a[0] if len(ks) == 1 else jnp.concatenate(a, axis=1), v)
            r_ref[...] = r_run

        @pl.when(p == 0)
        def _():
            blockmask = (_iota2((ht, width), 0) // t) == (_iota2((ht, width), 1) // LANES)
            qt = jnp.concatenate([q_ref[...]] * nh, axis=0)
            qb_ref[...] = jnp.where(blockmask, qt, 0.0).astype(BF16)
            acc_ref[...] = jnp.zeros_like(acc_ref)
            r_ref[...] = jnp.zeros_like(r_ref)
            pad = jnp.zeros((sk - t, width), F32)
            kpad = _bf(jnp.concatenate([kn_ref[...], pad], axis=0))
            vpad = _bf(jnp.concatenate([vn_ref[...], pad], axis=0))
            mask = _iota2((ht, sk), 1) < (_iota2((ht, sk), 0) % t)
            process([kpad], vpad, mask)

        process([pk_ref[s * sk:(s + 1) * sk, :] for s in range(nsub)], pv_ref[...], None)

        @pl.when(p == nblk - 1)
        def _():
            blockmask = (_iota2((ht, width), 0) // t) == (_iota2((ht, width), 1) // LANES)
            accm = jnp.where(blockmask, acc_ref[...], 0.0)
            out = accm[0:t, :]
            for hh in range(1, nh):
                out = out + accm[hh * t:(hh + 1) * t, :]
            o_ref[...] = out.astype(o_ref.dtype)

    past_spec = pl.BlockSpec((None, None, kb, width), lambda i, p: (layer, i, nblk - 1 - p, 0))
    return pl.pallas_call(
        kern,
        grid=(b, nblk),
        in_specs=[pl.BlockSpec((None, t, width), lambda i, p: (i, 0, 0)),
                  pl.BlockSpec((None, t, width), lambda i, p: (i, 0, 0)),
                  pl.BlockSpec((None, t, width), lambda i, p: (i, 0, v_blk0 // nh)),
                  pl.BlockSpec((ht, sk), lambda i, p: (0, 0)),
                  past_spec, past_spec],
        out_specs=pl.BlockSpec((None, t, width), lambda i, p: (i, 0, 0)),
        out_shape=jax.ShapeDtypeStruct((b, t, width), _act_dtype(t)),
        scratch_shapes=[pltpu.VMEM((ht, width), F32), pltpu.VMEM((ht, 1), F32), pltpu.VMEM((ht, width), BF16)],
        compiler_params=_cparams("parallel", "arbitrary"),
        name="sb_paged",
    )(qn3, kn3, proj3, bias_rows, past_k, past_v)


def _ffn_up(xn, w, layer, conv_w, buf, *, b, t):
    m, kdim = xn.shape
    f = w.shape[2] // 2
    kc = conv_w.shape[0]
    tm = _pick(t, 1024, 8)
    tn = _pick(f, 512)
    tpb = t // tm
    nfb = f // tn
    off = 8 - (kc - 1)
    sub = _pick(tm, 256, 8)
    assert m == b * t and t % tm == 0 and w.shape[1] == kdim

    def kern(x_ref, wa_ref, wb_ref, cw_ref, buf_ref, hid_ref, nb_ref, xs, wbf):
        i = pl.program_id(1)

        @pl.when(i == 0)
        def _():
            wbf[:, :tn] = _bf(wa_ref[...])
            wbf[:, tn:] = _bf(wb_ref[...])

        first = (i % tpb) == 0

        @pl.when(first)
        def _():
            xs[off:8, :] = buf_ref[...]

        @pl.when(jnp.logical_not(first))
        def _():
            xs[off:8, :] = xs[tm + off:tm + 8, :]

        def dots(s):
            ab = _dot(x_ref[s * sub:(s + 1) * sub, :], wbf[...])
            return ab[:, :tn], ab[:, tn:]

        def activate(s, a, gate):
            r0 = s * sub
            xs[8 + r0:8 + r0 + sub, :] = a
            y = xs[off + r0:off + r0 + sub, :] * cw_ref[0:1, :]
            for kk in range(1, kc):
                y = y + xs[off + kk + r0:off + kk + r0 + sub, :] * cw_ref[kk:kk + 1, :]
            hid_ref[r0:r0 + sub, :] = (_gelu_tanh(y) * gate).astype(hid_ref.dtype)

        nsub = tm // sub
        cur = dots(0)
        for s in range(1, nsub):
            nxt = dots(s)
            activate(s - 1, *cur)
            cur = nxt
        activate(nsub - 1, *cur)
        nb_ref[...] = xs[tm + off:tm + 8, :]

    return pl.pallas_call(
        kern,
        grid=(nfb, m // tm),
        in_specs=[pl.BlockSpec((tm, kdim), lambda j, i: (i, 0)),
                  pl.BlockSpec((None, kdim, tn), lambda j, i: (layer, 0, j)),
                  pl.BlockSpec((None, kdim, tn), lambda j, i: (layer, 0, nfb + j)),
                  pl.BlockSpec((kc, tn), lambda j, i: (0, j)),
                  pl.BlockSpec((None, kc - 1, tn), lambda j, i: (i // tpb, 0, j))],
        out_specs=[pl.BlockSpec((tm, tn), lambda j, i: (i, j)),
                   pl.BlockSpec((None, kc - 1, tn), lambda j, i: (i // tpb, 0, j))],
        out_shape=[jax.ShapeDtypeStruct((m, f), BF16), jax.ShapeDtypeStruct((b, kc - 1, f), F32)],
        scratch_shapes=[pltpu.VMEM((tm + 8, tn), F32), pltpu.VMEM((kdim, 2 * tn), BF16)],
        compiler_params=_cparams("arbitrary", "arbitrary"),
        name="ffn_up",
    )(xn, w, w, conv_w, buf)


def _ffn_act(u3, buf, conv_w):
    b, t, two_f = u3.shape
    f = two_f // 2
    kc = conv_w.shape[0]
    tc = _pick(f, 8192)
    nfb = f // tc
    off = 8 - (kc - 1)

    def kern(a_ref, b_ref, buf_ref, w_ref, hid_ref, nb_ref, xs):
        xs[off:8, :] = buf_ref[...]
        xs[8:8 + t, :] = a_ref[...]
        nb_ref[...] = xs[t + off:t + 8, :]
        y = xs[off:off + t, :] * w_ref[0:1, :]
        for kk in range(1, kc):
            y = y + xs[off + kk:off + kk + t, :] * w_ref[kk:kk + 1, :]
        hid_ref[...] = _gelu_tanh(y) * b_ref[...]

    return pl.pallas_call(
        kern,
        grid=(b, nfb),
        in_specs=[pl.BlockSpec((None, t, tc), lambda i, j: (i, 0, j)),
                  pl.BlockSpec((None, t, tc), lambda i, j: (i, 0, nfb + j)),
                  pl.BlockSpec((None, kc - 1, tc), lambda i, j: (i, 0, j)),
                  pl.BlockSpec((kc, tc), lambda i, j: (0, j))],
        out_specs=[pl.BlockSpec((None, t, tc), lambda i, j: (i, 0, j)),
                   pl.BlockSpec((None, kc - 1, tc), lambda i, j: (i, 0, j))],
        out_shape=[jax.ShapeDtypeStruct((b, t, f), F32), jax.ShapeDtypeStruct((b, kc - 1, f), F32)],
        scratch_shapes=[pltpu.VMEM((t + 8, tc), F32)],
        compiler_params=_cparams("parallel", "parallel"),
        name="ffn_act",
    )(u3, u3, buf, conv_w)


def _rot_tables(pos, dk):
    half = dk // 2
    inv = 1.0 / (RET_THETA ** (jnp.arange(half, dtype=F32) / half))
    ang = pos.astype(F32)[:, None] * inv[None, :]
    cos, sin = jnp.cos(ang), jnp.sin(ang)
    return jnp.concatenate([cos, cos], -1), jnp.concatenate([-sin, sin], -1)


def _trunk(x3, p4, pos, conv_a, rec_a, rec_b, rec_d, ffn_buf, W, dims, paged=None):
    b, t, d = x3.shape
    m = b * t
    (h_a, h_b, dv_b, h_c, h_d, dv_d, r_d) = dims
    qa = h_a * LANES
    conv_ch = 3 * qa
    qb = h_b * LANES
    vb = h_b * dv_b
    qc = h_c * LANES
    qd = h_d * LANES
    vd = h_d * dv_d
    depth = W["norm_mix"].shape[0]
    h2 = x3.reshape(m, d)
    cos_t, sin_t = _rot_tables(pos, LANES)
    log_gamma = jnp.log1p(-jnp.exp2(-5.0 - jnp.arange(h_b, dtype=F32)))
    ld_ret = jnp.broadcast_to(jnp.repeat(log_gamma, LANES)[None, None, :], (b, t, qb))
    conv_l, ra_l, rb_l, k_l, v_l, rd_l, ffn_l = [], [], [], [], [], [], []

    for i in range(depth):
        j = i // 2
        xn = _rms_cast(h2, W["norm_mix"][i])
        if i % 2 == 0:
            pre3 = _matmul(xn, W["w_in_even_pre"], j).reshape(b, t, conv_ch)
            proj3 = _matmul(xn, W["w_in_even_tail"], j).reshape(b, t, -1)
            sp3 = _matmul(xn, W["w_in_even_small"], j, tn_pref=LANES).reshape(b, t, LANES)
            qkv3, c_new = _conv_a_prep(pre3, conv_a[j], W["conv_a_w"][j], qa=qa, dk=LANES)
            o_a, sa_new = _gated_delta(qkv3, sp3, proj3, 0, W["alog_p"][j], W["dtb_p"][j],
                                       W["a_out_norm"][j].reshape(1, LANES), rec_a[j], nh=h_a)
            off_qb = qa
            o_b, sb_new = _gla(proj3, off_qb // LANES, (off_qb + qb) // LANES, (off_qb + 2 * qb) // dv_b,
                               (off_qb + 2 * qb + vb) // dv_b, W["b_gn"][j].reshape(h_b, 1, dv_b), rec_b[j],
                               nh=h_b, dv=dv_b, k_scale=LANES ** -0.5, norm="group",
                               rot=(cos_t, sin_t), ld3=ld_ret)
            h2 = _matmul_pair(o_a.astype(BF16).reshape(m, -1), o_b.astype(BF16).reshape(m, -1),
                              W["w_out_even"], j, h2)
            conv_l.append(c_new)
            ra_l.append(sa_new)
            rb_l.append(sb_new)
        else:
            proj3 = _matmul(xn, W["w_in_odd_main"], j).reshape(b, t, -1)
            sp3 = _matmul(xn, W["w_in_odd_small"], j, tn_pref=LANES).reshape(b, t, LANES)
            qn3 = _head_rms(proj3, 0, W["c_q_norm"][j], nh=h_c)
            kn3 = _head_rms(proj3, h_c, W["c_k_norm"][j], nh=h_c)
            if paged is None:
                o_c = _sb_prompt(qn3, kn3, proj3, 2 * h_c, W["c_logit_bias"][j], nh=h_c)
            else:
                past_k, past_v = paged
                o_c = _sb_paged(qn3, kn3, proj3, 2 * h_c, W["c_logit_bias"][j], past_k, past_v, j, nh=h_c)
            off_qd = 3 * qc
            o_d, sd_new = _gla(proj3, off_qd // LANES, (off_qd + qd) // LANES, (off_qd + 2 * qd) // dv_d,
                               (off_qd + 2 * qd + vd) // dv_d, W["d_gain"][j], rec_d[j],
                               nh=h_d, dv=dv_d, k_scale=LANES ** -0.5, norm="rms",
                               lowrank=(sp3, W["d_w2p"][j], W["d_alpha_b"][j].reshape(1, qd)))
            h2 = _matmul_pair(o_c.astype(BF16).reshape(m, -1), o_d.astype(BF16).reshape(m, -1),
                              W["w_out_odd"], j, h2)
            k_l.append(kn3.reshape(b, t, h_c, LANES))
            v_l.append(proj3[:, :, 2 * qc:3 * qc].reshape(b, t, h_c, LANES))
            rd_l.append(sd_new)
        xn = _rms_cast(h2, W["norm_ffn"][i])
        if t % BF16_SUBLANES == 0:
            hid, fb = _ffn_up(xn, W["w_ffn_in"], i, W["ffn_conv_w"][i], ffn_buf[i], b=b, t=t)
        else:
            u3 = _matmul(xn, W["w_ffn_in"], i).reshape(b, t, -1)
            hid, fb = _ffn_act(u3, ffn_buf[i], W["ffn_conv_w"][i])
            hid = hid.astype(BF16).reshape(m, -1)
        f = hid.shape[-1]
        tk_f = f // 2 if (f // 2) % LANES == 0 else f
        h2 = _matmul(hid, W["w_ffn_out"], i, res=h2, tk=tk_f)
        ffn_l.append(fb)
        xn = _rms_cast(h2, W["ple_norm"][i])
        h2 = _matmul(xn, W["w_ple_gate"], i, ple=(h2, p4[i].reshape(m, -1).astype(BF16), W["w_ple_in"]))
    return (h2.reshape(b, t, d), jnp.stack(conv_l), jnp.stack(ra_l), jnp.stack(rb_l), jnp.stack(k_l),
            jnp.stack(v_l), jnp.stack(rd_l), jnp.stack(ffn_l))


def kernel(x_prompt, x_sample, state_a_conv, state_a_rec, state_b_rec, cache_k, cache_v, state_d_rec, state_ffn_conv, page_table, p_prompt, p_sample, norm_mix, w_in_even, conv_a_w, a_log, dt_bias, a_out_norm, b_gn, w_out_even, w_in_odd, c_q_norm, c_k_norm, c_logit_bias, d_alpha_w2, d_alpha_b, d_out_norm, w_out_odd, norm_ffn, w_ffn_in, ffn_conv_w, w_ffn_out, ple_norm, w_ple_gate, w_ple_in):
    n_even, h_a = a_log.shape
    n_odd, h_c = c_logit_bias.shape
    h_b, dv_b = state_b_rec.shape[2], state_b_rec.shape[4]
    h_d, dv_d = state_d_rec.shape[2], state_d_rec.shape[4]
    r_d = d_alpha_w2.shape[1]
    depth = norm_mix.shape[0]
    conv_ch = conv_a_w.shape[2]
    qa = h_a * LANES
    qc = h_c * LANES
    qd = h_d * LANES
    assert conv_ch == 3 * qa and state_a_rec.shape[3:] == (LANES, LANES)
    assert state_b_rec.shape[3] == LANES and state_d_rec.shape[3] == LANES and cache_k.shape[4] == LANES
    assert 2 * h_a <= LANES and r_d <= LANES
    dims = (h_a, h_b, dv_b, h_c, h_d, dv_d, r_d)

    w_tail_e = w_in_even[:, :, conv_ch + 2 * h_a:].astype(BF16)
    w_small_e = jnp.pad(w_in_even[:, :, conv_ch:conv_ch + 2 * h_a],
                        ((0, 0), (0, 0), (0, LANES - 2 * h_a))).astype(BF16)
    n_main_o = 3 * qc + 2 * qd + 2 * h_d * dv_d
    w_small_o = jnp.pad(w_in_odd[:, :, n_main_o:], ((0, 0), (0, 0), (0, LANES - r_d))).astype(BF16)
    pad_ab = lambda v: jnp.pad(v, ((0, 0), (h_a, LANES - 2 * h_a)))[:, None, :]
    W = {
        "norm_mix": norm_mix, "norm_ffn": norm_ffn, "ple_norm": ple_norm,
        "w_in_even_pre": w_in_even[:, :, :conv_ch].astype(BF16), "w_in_even_tail": w_tail_e,
        "w_in_even_small": w_small_e,
        "w_in_odd_main": w_in_odd[:, :, :n_main_o].astype(BF16), "w_in_odd_small": w_small_o,
        "conv_a_w": conv_a_w, "alog_p": pad_ab(a_log), "dtb_p": pad_ab(dt_bias),
        "a_out_norm": a_out_norm, "b_gn": b_gn,
        "w_out_even": w_out_even, "w_out_odd": w_out_odd,
        "c_q_norm": c_q_norm, "c_k_norm": c_k_norm, "c_logit_bias": c_logit_bias,
        "d_w2p": jnp.pad(d_alpha_w2, ((0, 0), (0, LANES - r_d), (0, 0))), "d_alpha_b": d_alpha_b,
        "d_gain": jnp.broadcast_to(d_out_norm[:, None, None, :], (n_odd, h_d, 1, dv_d)),
        "w_ffn_in": w_ffn_in, "ffn_conv_w": ffn_conv_w, "w_ffn_out": w_ffn_out.astype(BF16),
        "w_ple_gate": w_ple_gate, "w_ple_in": w_ple_in.astype(BF16),
    }

    bp, tp = x_prompt.shape[0], x_prompt.shape[1]
    kconv = conv_a_w.shape[1]
    kffn = ffn_conv_w.shape[1]
    f = w_ffn_out.shape[1]
    zeros = lambda *s: jnp.zeros(s, F32)
    out_p = _trunk(x_prompt, p_prompt, jnp.arange(tp, dtype=jnp.int32),
                   zeros(n_even, bp, kconv - 1, conv_ch), zeros(n_even, bp, h_a, LANES, LANES),
                   zeros(n_even, bp, h_b, LANES, dv_b), zeros(n_odd, bp, h_d, LANES, dv_d),
                   zeros(depth, bp, kffn - 1, f), W, dims)

    ts = x_sample.shape[1]
    past_len = page_table.shape[1] * cache_k.shape[2]
    pos_s = past_len + jnp.arange(ts, dtype=jnp.int32)
    out_s = _trunk(x_sample, p_sample, pos_s, state_a_conv, state_a_rec, state_b_rec, state_d_rec,
                   state_ffn_conv, W, dims, paged=_gather_pages(cache_k, cache_v, page_table))

    res = []
    for a, s in zip(out_p, out_s):
        res += [a, s]
    return tuple(res)
```
